```python
import jax, jax.numpy as jnp
from jax import lax
import numpy as np

D_MODEL = 2048
BATCH = 4
SEQ = 4096
DEPTH = 2

CHUNK = 64
N_A_LAYERS = DEPTH // 2
N_B_LAYERS = DEPTH - N_A_LAYERS
EPS = 1e-6

A_INNER = 2 * D_MODEL
A_HEADS = 8
A_DV = A_INNER // A_HEADS
A_QK = A_INNER // 2
A_DK = A_QK // A_HEADS
CONV_W = 4
A_COLS = 2 * A_QK + 3 * A_INNER + 2 * A_HEADS
A_V0 = 2 * A_QK
A_O0 = A_V0 + A_INNER
A_Z0 = A_O0 + A_INNER
A_G0 = A_Z0 + A_INNER

B_HEADS = 16
B_DH = D_MODEL // B_HEADS
B_WIDTH = B_HEADS * B_DH
LEFT_CHUNKS = 8
BAND = (LEFT_CHUNKS + 1) * CHUNK
MAX_REL = 128

kernel_name = "yoco_mlstm_chunk_relpos_hybrid"

F32 = jnp.float32


def rmsnorm(x, g):
    xf = x.astype(F32)
    return xf * lax.rsqrt(jnp.mean(xf * xf, axis=-1, keepdims=True) + EPS) * g.astype(F32)


def modulate(hn, shift, scale):
    return hn * (1.0 + scale[:, None, :]) + shift[:, None, :]


def causal_conv(u, w, b):
    S = u.shape[1]
    up = jnp.pad(u, ((0, 0), (CONV_W - 1, 0), (0, 0)))
    out = b
    for j in range(CONV_W):
        out = out + up[:, j:j + S] * w[j]
    return out


def mlstm_chunkwise(q, k, v, li, lf):
    Bn, H, S, DK = q.shape
    DV = v.shape[-1]
    NC = S // CHUNK

    def to_chunks(a):
        return jnp.moveaxis(a.reshape(Bn, H, NC, CHUNK, *a.shape[3:]), 2, 0)

    qc, kc, vc, lic, lfc = (to_chunks(a) for a in (q, k * (DK ** -0.5), v, li, lf))
    causal = jnp.tril(jnp.ones((CHUNK, CHUNK), bool))

    def step(carry, inp):
        C, n, m = carry
        qi, ki, vi, lii, lfi = inp
        b = jnp.cumsum(lfi, axis=-1)
        dmat = b[..., :, None] - b[..., None, :] + lii[..., None, :]
        dmat = jnp.where(causal, dmat, -jnp.inf)
        inter = b + m[..., None]
        m_t = jnp.maximum(inter, jnp.max(dmat, axis=-1))
        w_intra = jnp.exp(dmat - m_t[..., None])
        w_inter = jnp.exp(inter - m_t)
        s = jnp.einsum('bhtd,bhsd->bhts', qi, ki) * w_intra
        num = (w_inter[..., None] * jnp.einsum('bhtd,bhde->bhte', qi, C)
               + jnp.einsum('bhts,bhse->bhte', s, vi))
        den = w_inter * jnp.einsum('bhtd,bhd->bht', qi, n) + jnp.sum(s, axis=-1)
        h = num / jnp.maximum(jnp.abs(den), jnp.exp(-m_t))[..., None]
        b_last = b[..., -1]
        g_s = b_last[..., None] - b + lii
        m_new = jnp.maximum(b_last + m, jnp.max(g_s, axis=-1))
        decay = jnp.exp(b_last + m - m_new)
        ws = jnp.exp(g_s - m_new[..., None])
        C_new = decay[..., None, None] * C + jnp.einsum('bhs,bhsd,bhse->bhde', ws, ki, vi)
        n_new = decay[..., None] * n + jnp.einsum('bhs,bhsd->bhd', ws, ki)
        return (C_new, n_new, m_new), h

    init = (jnp.zeros((Bn, H, DK, DV), F32), jnp.zeros((Bn, H, DK), F32), jnp.zeros((Bn, H), F32))
    _, hs = lax.scan(step, init, (qc, kc, vc, lic, lfc))
    return jnp.moveaxis(hs, 0, 2).reshape(Bn, H, S, DV)


def mlstm_mixer(h, w_in, conv_w, conv_b, gate_b, g_head, w_out):
    Bn, S, _ = h.shape
    proj = h @ w_in
    qk = jax.nn.silu(causal_conv(proj[..., :A_V0].astype(F32), conv_w.astype(F32), conv_b.astype(F32)))
    v = proj[..., A_V0:A_O0]
    o = proj[..., A_O0:A_Z0]
    z = proj[..., A_Z0:A_G0]
    gates = proj[..., A_G0:].astype(F32) + gate_b.astype(F32)

    def heads(a, d):
        return a.astype(F32).reshape(Bn, S, A_HEADS, d).transpose(0, 2, 1, 3)

    li = gates[..., :A_HEADS].transpose(0, 2, 1)
    lf = jax.nn.log_sigmoid(gates[..., A_HEADS:]).transpose(0, 2, 1)
    ht = mlstm_chunkwise(heads(qk[..., :A_QK], A_DK), heads(qk[..., A_QK:], A_DK), heads(v, A_DV), li, lf)
    ht = rmsnorm(ht.transpose(0, 2, 1, 3), g_head.reshape(A_HEADS, A_DV)).reshape(Bn, S, A_INNER)
    y = jax.nn.sigmoid(o.astype(F32)) * ht * jax.nn.silu(z.astype(F32))
    return y.astype(h.dtype) @ w_out


def chunk_band_attention(q, k, v, rel_table):
    Bn, S, H, Dh = q.shape
    NC = S // CHUNK
    pad = LEFT_CHUNKS * CHUNK
    q = q.astype(F32)
    kp = jnp.pad(k.astype(F32), ((0, 0), (pad, 0), (0, 0), (0, 0)))
    vp = jnp.pad(v.astype(F32), ((0, 0), (pad, 0), (0, 0), (0, 0)))
    qc = jnp.moveaxis(q.reshape(Bn, NC, CHUNK, H, Dh), 1, 0)
    r = jnp.arange(CHUNK)[:, None]
    idx = jnp.arange(BAND)[None, :]
    bucket = jnp.clip(pad + r - idx, -MAX_REL, MAX_REL) + MAX_REL
    bias = rel_table.astype(F32)[:, bucket]
    key_off = jnp.arange(BAND) - pad
    scale = Dh ** -0.5

    def one_chunk(args):
        ci, qi = args
        start = ci * CHUNK
        kb = lax.dynamic_slice_in_dim(kp, start, BAND, axis=1)
        vb = lax.dynamic_slice_in_dim(vp, start, BAND, axis=1)
        s = jnp.einsum('blhd,bkhd->bhlk', qi, kb) * scale + bias
        s = jnp.where((start + key_off) >= 0, s, -jnp.inf)
        p = jax.nn.softmax(s, axis=-1)
        return jnp.einsum('bhlk,bkhd->blhd', p, vb)

    out = lax.map(one_chunk, (jnp.arange(NC), qc))
    return jnp.moveaxis(out, 0, 1).reshape(Bn, S, H, Dh)


def band_attention_mixer(h, k, v, w_in, rel_table, w_out):
    Bn, S, _ = h.shape
    qz = h @ w_in
    q = qz[..., :B_WIDTH].reshape(Bn, S, B_HEADS, B_DH)
    z = qz[..., B_WIDTH:]
    o = chunk_band_attention(q, k, v, rel_table)
    y = o.reshape(Bn, S, B_WIDTH) * jax.nn.silu(z.astype(F32))
    return y.astype(h.dtype) @ w_out


def setup_inputs(seed: int = 0) -> dict:
    key = jax.random.key(seed)
    ks = jax.random.split(key, 24)

    def nrm(k, shape, fan):
        return jax.random.normal(k, shape, F32) * (fan ** -0.5)

    def small(k, shape, s):
        return jax.random.normal(k, shape, F32) * s

    x = jax.random.normal(ks[0], (BATCH, SEQ, D_MODEL), F32)
    c = jax.random.normal(ks[1], (BATCH, D_MODEL), F32)
    ada_w = 0.5 * nrm(ks[2], (DEPTH, D_MODEL, 3 * D_MODEL), D_MODEL)
    ada_b = small(ks[3], (DEPTH, 3 * D_MODEL), 0.02)
    g_pre = 1.0 + small(ks[4], (DEPTH, D_MODEL), 0.05)
    g_post = 1.0 + small(ks[5], (DEPTH, D_MODEL), 0.05)
    a_w_in = nrm(ks[6], (N_A_LAYERS, D_MODEL, A_COLS), D_MODEL)
    a_conv_w = nrm(ks[7], (N_A_LAYERS, CONV_W, 2 * A_QK), CONV_W)
    a_conv_b = small(ks[8], (N_A_LAYERS, 2 * A_QK), 0.02)
    i_bias = small(ks[9], (N_A_LAYERS, A_HEADS), 0.1)
    f_bias = jnp.linspace(3.0, 6.0, A_HEADS, dtype=F32)[None, :] + small(ks[10], (N_A_LAYERS, A_HEADS), 0.1)
    a_gate_b = jnp.concatenate([i_bias, f_bias], axis=-1)
    a_g_head = 1.0 + small(ks[11], (N_A_LAYERS, A_INNER), 0.05)
    a_w_out = nrm(ks[12], (N_A_LAYERS, A_INNER, D_MODEL), A_INNER)
    kv_ada_w = 0.5 * nrm(ks[13], (D_MODEL, 2 * D_MODEL), D_MODEL)
    kv_ada_b = small(ks[14], (2 * D_MODEL,), 0.02)
    kv_g = 1.0 + small(ks[15], (D_MODEL,), 0.05)
    kv_w = nrm(ks[16], (D_MODEL, 2 * B_WIDTH), D_MODEL)
    b_w_in = nrm(ks[17], (N_B_LAYERS, D_MODEL, 2 * B_WIDTH), D_MODEL)
    b_rel = small(ks[18], (N_B_LAYERS, B_HEADS, 2 * MAX_REL + 1), 0.5)
    b_w_out = nrm(ks[19], (N_B_LAYERS, B_WIDTH, D_MODEL), B_WIDTH)
    return {"x": x, "c": c, "ada_w": ada_w, "ada_b": ada_b, "g_pre": g_pre, "g_post": g_post,
            "a_w_in": a_w_in, "a_conv_w": a_conv_w, "a_conv_b": a_conv_b, "a_gate_b": a_gate_b,
            "a_g_head": a_g_head, "a_w_out": a_w_out, "kv_ada_w": kv_ada_w, "kv_ada_b": kv_ada_b,
            "kv_g": kv_g, "kv_w": kv_w, "b_w_in": b_w_in, "b_rel": b_rel, "b_w_out": b_w_out}


def reference(x, c, ada_w, ada_b, g_pre, g_post, a_w_in, a_conv_w, a_conv_b, a_gate_b, a_g_head,
              a_w_out, kv_ada_w, kv_ada_b, kv_g, kv_w, b_w_in, b_rel, b_w_out):
    Bn, S, _ = x.shape
    sc = jax.nn.silu(c.astype(F32))
    k_sh = None
    v_sh = None
    for l in range(DEPTH):
        mod = sc @ ada_w[l].astype(F32) + ada_b[l].astype(F32)
        shift, scale, gate = mod[:, :D_MODEL], mod[:, D_MODEL:2 * D_MODEL], mod[:, 2 * D_MODEL:]
        h = modulate(rmsnorm(x, g_pre[l]), shift, scale).astype(x.dtype)
        if l < N_A_LAYERS:
            y = mlstm_mixer(h, a_w_in[l], a_conv_w[l], a_conv_b[l], a_gate_b[l], a_g_head[l], a_w_out[l])
        else:
            if l == N_A_LAYERS:
                kmod = sc @ kv_ada_w.astype(F32) + kv_ada_b.astype(F32)
                hkv = modulate(rmsnorm(x, kv_g), kmod[:, :D_MODEL], kmod[:, D_MODEL:]).astype(x.dtype)
                kv = hkv @ kv_w
                k_sh = kv[..., :B_WIDTH].reshape(Bn, S, B_HEADS, B_DH)
                v_sh = kv[..., B_WIDTH:].reshape(Bn, S, B_HEADS, B_DH)
            j = l - N_A_LAYERS
            y = band_attention_mixer(h, k_sh, v_sh, b_w_in[j], b_rel[j], b_w_out[j])
        x = (x.astype(F32) + gate[:, None, :] * rmsnorm(y, g_post[l])).astype(x.dtype)
    return x
```

```python
import functools

import jax
import jax.numpy as jnp
from jax import lax
from jax.experimental import pallas as pl
from jax.experimental.pallas import tpu as pltpu

F32 = jnp.float32
BF16 = jnp.bfloat16

EPS = 1e-6
CHUNK = 64
A_HEADS = 8
A_DK = 256
A_DV = 512
A_QK = A_HEADS * A_DK
A_INNER = A_HEADS * A_DV
CONV_W = 4
B_HEADS = 16
B_DH = 128
B_WIDTH = B_HEADS * B_DH
LEFT_CHUNKS = 8
PAD = LEFT_CHUNKS * CHUNK
MAX_REL = 128
NEG = -1e30

LANES = 128
SUBLANES = 8

MLSTM_CHUNK = 128
ATTN_LQ = 128
ATTN_WK = PAD + ATTN_LQ
ATTN_NB = (PAD + ATTN_WK) // LANES

VMEM_LIMIT = 48 * 1024 * 1024


def _silu(v):
    return v * jax.nn.sigmoid(v)


def _log_sigmoid(v):
    return -(jnp.maximum(-v, 0.0) + jnp.log1p(jnp.exp(-jnp.abs(v))))


def _mod_kernel(c_ref, w_ref, b_ref, o_ref):
    sc = _silu(c_ref[...])
    o_ref[...] = jnp.dot(sc.astype(BF16), w_ref[...].astype(BF16),
                         preferred_element_type=F32) + b_ref[...]


def _mod_call(c8, w, b, layer=None):
    d = c8.shape[1]
    n = w.shape[-1]
    tn = 1024
    if layer is None:
        w_spec = pl.BlockSpec((d, tn), lambda j: (0, j))
        b_spec = pl.BlockSpec((1, tn), lambda j: (0, j))
        b2 = b.reshape(1, n)
    else:
        w_spec = pl.BlockSpec((None, d, tn), lambda j: (layer, 0, j))
        b_spec = pl.BlockSpec((None, 1, tn), lambda j: (layer, 0, j))
        b2 = b.reshape(b.shape[0], 1, n)
    return pl.pallas_call(
        _mod_kernel,
        grid=(n // tn,),
        in_specs=[pl.BlockSpec((SUBLANES, d), lambda j: (0, 0)), w_spec, b_spec],
        out_specs=pl.BlockSpec((SUBLANES, tn), lambda j: (0, j)),
        out_shape=jax.ShapeDtypeStruct((SUBLANES, n), F32),
        compiler_params=pltpu.CompilerParams(
            dimension_semantics=("arbitrary",), vmem_limit_bytes=VMEM_LIMIT),
        name="adaln_mod",
    )(c8, w, b2)


def _nmm_kernel(x_ref, g_ref, shift_ref, scale_ref, w_ref, *rest, with_gates):
    if with_gates:
        wg_ref, o_ref, og_ref, h_scr = rest
    else:
        o_ref, h_scr = rest

    @pl.when(pl.program_id(1) == 0)
    def _():
        x = x_ref[...]
        ms = jnp.mean(x * x, axis=-1, keepdims=True)
        hn = x * lax.rsqrt(ms + EPS) * g_ref[...]
        hb = (hn * (1.0 + scale_ref[...]) + shift_ref[...]).astype(BF16)
        h_scr[...] = hb
        if with_gates:
            og_ref[...] = jnp.dot(hb, wg_ref[...], preferred_element_type=F32)

    o_ref[...] = jnp.dot(h_scr[...], w_ref[...], preferred_element_type=F32).astype(o_ref.dtype)


def _nmm_call(x2, g, mod3, shift_blk, scale_blk, w, wg, seq, name):
    t, d = x2.shape
    n = w.shape[1]
    tm, tn = 1024, 1024
    per_batch = seq // tm
    in_specs = [
        pl.BlockSpec((tm, d), lambda i, j: (i, 0)),
        pl.BlockSpec((1, d), lambda i, j: (0, 0)),
        pl.BlockSpec((None, 1, d), lambda i, j: (i // per_batch, 0, shift_blk)),
        pl.BlockSpec((None, 1, d), lambda i, j: (i // per_batch, 0, scale_blk)),
        pl.BlockSpec((d, tn), lambda i, j: (0, j)),
    ]
    out_specs = pl.BlockSpec((tm, tn), lambda i, j: (i, j))
    out_shape = jax.ShapeDtypeStruct((t, n), BF16)
    args = [x2, g, mod3, mod3, w]
    if wg is not None:
        in_specs.append(pl.BlockSpec((d, LANES), lambda i, j: (0, 0)))
        out_specs = [out_specs, pl.BlockSpec((tm, LANES), lambda i, j: (i, 0))]
        out_shape = [out_shape, jax.ShapeDtypeStruct((t, LANES), F32)]
        args.append(wg)
    return pl.pallas_call(
        functools.partial(_nmm_kernel, with_gates=wg is not None),
        grid=(t // tm, n // tn),
        in_specs=in_specs,
        out_specs=out_specs,
        out_shape=out_shape,
        scratch_shapes=[pltpu.VMEM((tm, d), BF16)],
        compiler_params=pltpu.CompilerParams(
            dimension_semantics=("arbitrary", "arbitrary"), vmem_limit_bytes=VMEM_LIMIT),
        name=name,
    )(*args)


def _mlstm_kernel(gb_ref, q_ref, k_ref, v_ref, o_ref, z_ref, gr_ref, cwq_ref, cwk_ref,
                  cbq_ref, cbk_ref, gh_ref, y_ref, c_scr, n_scr, m_scr, hq_scr, hk_scr):
    head = pl.program_id(1)
    step = pl.program_id(2)
    L = q_ref.shape[0]

    @pl.when(step == 0)
    def _():
        c_scr[...] = jnp.zeros_like(c_scr)
        n_scr[...] = jnp.zeros_like(n_scr)
        m_scr[...] = jnp.zeros_like(m_scr)
        hq_scr[...] = jnp.zeros_like(hq_scr)
        hk_scr[...] = jnp.zeros_like(hk_scr)

    row8 = lax.broadcasted_iota(jnp.int32, (SUBLANES, q_ref.shape[1]), 0)

    def conv_silu(u, halo_scr, w_ref, b_ref):
        halo = halo_scr[...]
        acc = b_ref[...]
        for j in range(CONV_W - 1):
            s = CONV_W - 1 - j
            r = pltpu.roll(u, s, 0)
            first = jnp.where(row8 < s, pltpu.roll(halo, s, 0), r[:SUBLANES])
            acc = acc + jnp.concatenate([first, r[SUBLANES:]], axis=0) * w_ref[j:j + 1, :]
        acc = acc + u * w_ref[CONV_W - 1:CONV_W, :]
        halo_scr[...] = u[L - SUBLANES:]
        return _silu(acc)

    q = conv_silu(q_ref[...].astype(F32), hq_scr, cwq_ref, cbq_ref)
    k = conv_silu(k_ref[...].astype(F32), hk_scr, cwk_ref, cbk_ref) * (A_DK ** -0.5)
    vb = v_ref[...]

    li_row = gr_ref[0] + gb_ref[head]
    lf_row = _log_sigmoid(gr_ref[1] + gb_ref[A_HEADS + head])

    row = lax.broadcasted_iota(jnp.int32, (L, L), 0)
    col = lax.broadcasted_iota(jnp.int32, (L, L), 1)
    causal = col <= row
    eye = col == row
    b_col = jnp.sum(jnp.where(causal, lf_row, 0.0), axis=1, keepdims=True)
    b_row = jnp.sum(jnp.where(eye, b_col, 0.0), axis=0, keepdims=True)
    a_row = li_row - b_row
    dm = jnp.where(causal, b_col + a_row, NEG)
    m_prev = m_scr[...]
    inter = b_col + m_prev
    m_t = jnp.maximum(inter, jnp.max(dm, axis=1, keepdims=True))
    w_intra = jnp.exp(dm - m_t)
    w_inter = jnp.exp(inter - m_t)

    qb = q.astype(BF16)
    kb = k.astype(BF16)
    s = lax.dot_general(qb, kb, (((1,), (1,)), ((), ())), preferred_element_type=F32) * w_intra
    c_old = c_scr[...]
    n_old = n_scr[...]
    num = (w_inter * jnp.dot(qb, c_old.astype(BF16), preferred_element_type=F32)
           + jnp.dot(s.astype(BF16), vb, preferred_element_type=F32))
    den = (w_inter * jnp.sum(q * n_old, axis=1, keepdims=True)
           + jnp.sum(s, axis=1, keepdims=True))
    hh = num / jnp.maximum(jnp.abs(den), jnp.exp(-m_t))

    hn = hh * lax.rsqrt(jnp.mean(hh * hh, axis=-1, keepdims=True) + EPS) * gh_ref[...]
    y = jax.nn.sigmoid(o_ref[...].astype(F32)) * hn * _silu(z_ref[...].astype(F32))
    y_ref[...] = y.astype(y_ref.dtype)

    b_last = b_col[L - 1:L, :]
    g_row = b_last + a_row
    m_new = jnp.maximum(b_last + m_prev, jnp.max(g_row, axis=1, keepdims=True))
    decay = jnp.exp(b_last + m_prev - m_new)
    ws_row = jnp.exp(g_row - m_new)
    ws_col = jnp.sum(jnp.where(eye, ws_row, 0.0), axis=1, keepdims=True)
    kw = k * ws_col
    c_scr[...] = decay * c_old + lax.dot_general(
        kw.astype(BF16), vb, (((0,), (0,)), ((), ())), preferred_element_type=F32)
    n_scr[...] = decay * n_old + jnp.sum(kw, axis=0, keepdims=True)
    m_scr[...] = m_new


def _mlstm_call(proj, gates_row, gate_b, conv_w, conv_b, g_head):
    bn, seq, _ = proj.shape
    L = MLSTM_CHUNK
    kq = A_QK // A_DK
    v0 = 2 * A_QK // A_DV
    o0 = v0 + A_HEADS
    z0 = o0 + A_HEADS
    return pl.pallas_call(
        _mlstm_kernel,
        grid=(bn, A_HEADS, seq // L),
        in_specs=[
            pl.BlockSpec(memory_space=pltpu.SMEM),
            pl.BlockSpec((None, L, A_DK), lambda b, h, i: (b, i, h)),
            pl.BlockSpec((None, L, A_DK), lambda b, h, i: (b, i, kq + h)),
            pl.BlockSpec((None, L, A_DV), lambda b, h, i: (b, i, v0 + h)),
            pl.BlockSpec((None, L, A_DV), lambda b, h, i: (b, i, o0 + h)),
            pl.BlockSpec((None, L, A_DV), lambda b, h, i: (b, i, z0 + h)),
            pl.BlockSpec((None, 2, None, 1, L), lambda b, h, i: (b, 0, h, 0, i)),
            pl.BlockSpec((CONV_W, A_DK), lambda b, h, i: (0, h)),
            pl.BlockSpec((CONV_W, A_DK), lambda b, h, i: (0, kq + h)),
            pl.BlockSpec((1, A_DK), lambda b, h, i: (0, h)),
            pl.BlockSpec((1, A_DK), lambda b, h, i: (0, kq + h)),
            pl.BlockSpec((1, A_DV), lambda b, h, i: (0, h)),
        ],
        out_specs=pl.BlockSpec((None, L, A_DV), lambda b, h, i: (b, i, h)),
        out_shape=jax.ShapeDtypeStruct((bn, seq, A_INNER), BF16),
        scratch_shapes=[
            pltpu.VMEM((A_DK, A_DV), F32),
            pltpu.VMEM((1, A_DK), F32),
            pltpu.VMEM((1, 1), F32),
            pltpu.VMEM((SUBLANES, A_DK), F32),
            pltpu.VMEM((SUBLANES, A_DK), F32),
        ],
        compiler_params=pltpu.CompilerParams(
            dimension_semantics=("arbitrary", "arbitrary", "arbitrary"), vmem_limit_bytes=VMEM_LIMIT),
        name="mlstm_core",
    )(gate_b, proj, proj, proj, proj, proj, gates_row, conv_w, conv_w, conv_b, conv_b, g_head)


def _out_kernel(y_ref, w_ref, x_ref, g_ref, gate_ref, o_ref):
    y = jnp.dot(y_ref[...], w_ref[...], preferred_element_type=F32)
    yn = y * lax.rsqrt(jnp.mean(y * y, axis=-1, keepdims=True) + EPS) * g_ref[...]
    o_ref[...] = x_ref[...] + gate_ref[...] * yn


def _out_call(y2, w, x2, g, mod3, gate_blk, seq, name):
    t, kdim = y2.shape
    d = w.shape[1]
    tm = 512
    per_batch = seq // tm
    return pl.pallas_call(
        _out_kernel,
        grid=(t // tm,),
        in_specs=[
            pl.BlockSpec((tm, kdim), lambda i: (i, 0)),
            pl.BlockSpec((kdim, d), lambda i: (0, 0), pipeline_mode=pl.Buffered(1)),
            pl.BlockSpec((tm, d), lambda i: (i, 0)),
            pl.BlockSpec((1, d), lambda i: (0, 0)),
            pl.BlockSpec((None, 1, d), lambda i: (i // per_batch, 0, gate_blk)),
        ],
        out_specs=pl.BlockSpec((tm, d), lambda i: (i, 0)),
        out_shape=jax.ShapeDtypeStruct((t, d), F32),
        compiler_params=pltpu.CompilerParams(
            dimension_semantics=("arbitrary",), vmem_limit_bytes=VMEM_LIMIT),
        name=name,
    )(y2, w, x2, g, mod3)


def _bias_kernel(tbl_ref, o_ref):
    head = pl.program_id(0)
    nb, lq, _ = o_ref.shape
    width = nb * LANES + lq
    u = lax.broadcasted_iota(jnp.int32, (SUBLANES, width), 1)
    u = jnp.where(u >= nb * LANES, u - width, u)
    idx = jnp.clip(PAD - u, -MAX_REL, MAX_REL) + MAX_REL

    def body(c, base):
        return jnp.where(idx == c, tbl_ref[head, c], base)

    base = lax.fori_loop(0, 2 * MAX_REL + 1, body, jnp.zeros((SUBLANES, width), F32))
    tile = jnp.concatenate([base] * (lq // SUBLANES), axis=0)
    r = lax.broadcasted_iota(jnp.int32, (lq, width), 0)
    tile = pltpu.roll(tile, 0, 1, stride=1, stride_axis=0)
    uu = lax.broadcasted_iota(jnp.int32, (lq, width), 1)
    cq = r // CHUNK
    ck = uu // CHUNK - LEFT_CHUNKS
    visible = (ck <= cq) & (ck >= cq - LEFT_CHUNKS)
    tile = jnp.where(visible, tile, NEG)
    for j in range(nb):
        o_ref[j] = tile[:, j * LANES:(j + 1) * LANES]


def _bias_call(rel_table):
    return pl.pallas_call(
        _bias_kernel,
        grid=(B_HEADS,),
        in_specs=[pl.BlockSpec(memory_space=pltpu.SMEM)],
        out_specs=pl.BlockSpec((None, ATTN_NB, ATTN_LQ, LANES), lambda h: (h, 0, 0, 0)),
        out_shape=jax.ShapeDtypeStruct((B_HEADS, ATTN_NB, ATTN_LQ, LANES), F32),
        compiler_params=pltpu.CompilerParams(
            dimension_semantics=("arbitrary",), vmem_limit_bytes=VMEM_LIMIT),
        name="relpos_bias",
    )(rel_table)


def _attn_kernel(q_ref, z_ref, k_ref, v_ref, wb_ref, y_ref):
    lq = q_ref.shape[0]
    wk = PAD + lq
    start = pl.program_id(2) * lq
    wstart = pl.multiple_of(jnp.maximum(start - PAD, 0), LANES)
    off = jnp.maximum(PAD - start, 0) // LANES
    kwin = k_ref[pl.ds(wstart, wk), :]
    vwin = v_ref[pl.ds(wstart, wk), :]
    s = lax.dot_general(q_ref[...], kwin, (((1,), (1,)), ((), ())), preferred_element_type=F32)
    bias = jnp.concatenate([wb_ref[off + j] for j in range(wk // LANES)], axis=1)
    s = s * (B_DH ** -0.5) + bias
    m = jnp.max(s, axis=1, keepdims=True)
    p = jnp.exp(s - m)
    denom = jnp.sum(p, axis=1, keepdims=True)
    o = jnp.dot(p.astype(BF16), vwin, preferred_element_type=F32) / denom
    y_ref[...] = (o * _silu(z_ref[...].astype(F32))).astype(y_ref.dtype)


def _attn_call(qz, kv, wb):
    bn, seq, _ = qz.shape
    lq = ATTN_LQ
    return pl.pallas_call(
        _attn_kernel,
        grid=(bn, B_HEADS, seq // lq),
        in_specs=[
            pl.BlockSpec((None, lq, B_DH), lambda b, h, i: (b, i, h)),
            pl.BlockSpec((None, lq, B_DH), lambda b, h, i: (b, i, B_HEADS + h)),
            pl.BlockSpec((None, seq, B_DH), lambda b, h, i: (b, 0, h)),
            pl.BlockSpec((None, seq, B_DH), lambda b, h, i: (b, 0, B_HEADS + h)),
            pl.BlockSpec((None, ATTN_NB, lq, LANES), lambda b, h, i: (h, 0, 0, 0)),
        ],
        out_specs=pl.BlockSpec((None, lq, B_DH), lambda b, h, i: (b, i, h)),
        out_shape=jax.ShapeDtypeStruct((bn, seq, B_WIDTH), BF16),
        compiler_params=pltpu.CompilerParams(
            dimension_semantics=("arbitrary", "arbitrary", "arbitrary"), vmem_limit_bytes=VMEM_LIMIT),
        name="band_attention",
    )(qz, qz, kv, kv, wb)


def kernel(x, c, ada_w, ada_b, g_pre, g_post, a_w_in, a_conv_w, a_conv_b, a_gate_b, a_g_head, a_w_out,
           kv_ada_w, kv_ada_b, kv_g, kv_w, b_w_in, b_rel, b_w_out):
    bn, seq, d = x.shape
    t = bn * seq
    assert bn <= SUBLANES and seq % 1024 == 0 and seq >= PAD + ATTN_LQ
    x2 = x.reshape(t, d)
    c8 = jnp.pad(c.astype(F32), ((0, SUBLANES - bn), (0, 0)))

    mod0 = _mod_call(c8, ada_w, ada_b, layer=0).reshape(SUBLANES, 1, 3 * d)
    w_in = a_w_in[0]
    n_main = 2 * A_QK + 3 * A_INNER
    w_main = w_in[:, :n_main].astype(BF16)
    w_gate = jnp.pad(w_in[:, n_main:], ((0, 0), (0, LANES - 2 * A_HEADS))).astype(BF16)
    proj, gates = _nmm_call(x2, g_pre[0].reshape(1, d), mod0, 0, 1, w_main, w_gate, seq, "mlstm_in_proj")
    gates_row = gates[:, :2 * A_HEADS].reshape(bn, seq, 2, A_HEADS).transpose(0, 2, 3, 1)
    gates_row = gates_row.reshape(bn, 2, A_HEADS, 1, seq)
    y0 = _mlstm_call(proj.reshape(bn, seq, n_main), gates_row, a_gate_b[0], a_conv_w[0],
                     a_conv_b[0].reshape(1, 2 * A_QK), a_g_head[0].reshape(1, A_INNER))
    x2 = _out_call(y0.reshape(t, A_INNER), a_w_out[0].astype(BF16), x2, g_post[0].reshape(1, d),
                   mod0, 2, seq, "mlstm_out_proj")

    mod1 = _mod_call(c8, ada_w, ada_b, layer=1).reshape(SUBLANES, 1, 3 * d)
    kmod = _mod_call(c8, kv_ada_w, kv_ada_b).reshape(SUBLANES, 1, 2 * d)
    qz = _nmm_call(x2, g_pre[1].reshape(1, d), mod1, 0, 1, b_w_in[0].astype(BF16), None, seq, "attn_in_proj")
    kv = _nmm_call(x2, kv_g.reshape(1, d), kmod, 0, 1, kv_w.astype(BF16), None, seq, "attn_kv_proj")
    wb = _bias_call(b_rel[0])
    y1 = _attn_call(qz.reshape(bn, seq, 2 * B_WIDTH), kv.reshape(bn, seq, 2 * B_WIDTH), wb)
    x2 = _out_call(y1.reshape(t, B_WIDTH), b_w_out[0].astype(BF16), x2, g_post[1].reshape(1, d),
                   mod1, 2, seq, "attn_out_proj")
    return x2.reshape(bn, seq, d)
```

```python
import functools

import jax
import jax.numpy as jnp
from jax import lax
from jax.experimental import pallas as pl
from jax.experimental.pallas import tpu as pltpu

F32 = jnp.float32
BF16 = jnp.bfloat16

EPS = 1e-6
CHUNK = 64
A_HEADS = 8
A_DK = 256
A_DV = 512
A_QK = A_HEADS * A_DK
A_INNER = A_HEADS * A_DV
CONV_W = 4
B_HEADS = 16
B_DH = 128
B_WIDTH = B_HEADS * B_DH
LEFT_CHUNKS = 8
PAD = LEFT_CHUNKS * CHUNK
MAX_REL = 128
NEG = -1e30

LANES = 128
SUBLANES = 8

LOG2E = 1.4426950408889634

MLSTM_CHUNK = 256
ATTN_LQ = 128
ATTN_HPG = 8
ATTN_WK = PAD + ATTN_LQ
ATTN_NB = (PAD + ATTN_WK) // LANES

VMEM_LIMIT = 48 * 1024 * 1024


def _sigmoid(v):
    return 0.5 * jnp.tanh(0.5 * v) + 0.5


def _silu(v):
    hv = 0.5 * v
    return hv * jnp.tanh(hv) + hv


def _log_sigmoid(v):
    return -(jnp.maximum(-v, 0.0) + jnp.log1p(jnp.exp(-jnp.abs(v))))


def _mod_kernel(c_ref, w_ref, b_ref, o_ref):
    sc = _silu(c_ref[...])
    o_ref[...] = jnp.dot(sc.astype(BF16), w_ref[...].astype(BF16),
                         preferred_element_type=F32) + b_ref[...]


def _mod_call(c8, w, b, layer=None):
    d = c8.shape[1]
    n = w.shape[-1]
    tn = 1024
    if layer is None:
        w_spec = pl.BlockSpec((d, tn), lambda j: (0, j))
        b_spec = pl.BlockSpec((1, tn), lambda j: (0, j))
        b2 = b.reshape(1, n)
    else:
        w_spec = pl.BlockSpec((None, d, tn), lambda j: (layer, 0, j))
        b_spec = pl.BlockSpec((None, 1, tn), lambda j: (layer, 0, j))
        b2 = b.reshape(b.shape[0], 1, n)
    return pl.pallas_call(
        _mod_kernel,
        grid=(n // tn,),
        in_specs=[pl.BlockSpec((SUBLANES, d), lambda j: (0, 0)), w_spec, b_spec],
        out_specs=pl.BlockSpec((SUBLANES, tn), lambda j: (0, j)),
        out_shape=jax.ShapeDtypeStruct((SUBLANES, n), F32),
        compiler_params=pltpu.CompilerParams(
            dimension_semantics=("arbitrary",), vmem_limit_bytes=VMEM_LIMIT),
        name="adaln_mod",
    )(c8, w, b2)


def _nmm_kernel(x_ref, g_ref, shift_ref, scale_ref, w_ref, *rest, with_gates, lead_tiles, lead_scale):
    if with_gates:
        wg_ref, o_ref, og_ref, h_scr = rest
    else:
        o_ref, h_scr = rest

    @pl.when(pl.program_id(1) == 0)
    def _():
        x = x_ref[...]
        ms = jnp.mean(x * x, axis=-1, keepdims=True)
        hn = x * lax.rsqrt(ms + EPS) * g_ref[...]
        hb = (hn * (1.0 + scale_ref[...]) + shift_ref[...]).astype(BF16)
        h_scr[...] = hb
        if with_gates:
            og_ref[...] = jnp.dot(hb, wg_ref[...], preferred_element_type=F32)

    acc = jnp.dot(h_scr[...], w_ref[...], preferred_element_type=F32)
    if lead_tiles:
        acc = acc * jnp.where(pl.program_id(1) < lead_tiles, lead_scale, 1.0)
    o_ref[...] = acc.astype(o_ref.dtype)


def _nmm_call(x2, g, mod3, shift_blk, scale_blk, w, wg, seq, name, lead_cols=0, lead_scale=1.0):
    t, d = x2.shape
    n = w.shape[1]
    tm, tn = 1024, 1024
    assert lead_cols % tn == 0
    per_batch = seq // tm
    in_specs = [
        pl.BlockSpec((tm, d), lambda i, j: (i, 0)),
        pl.BlockSpec((1, d), lambda i, j: (0, 0)),
        pl.BlockSpec((None, 1, d), lambda i, j: (i // per_batch, 0, shift_blk)),
        pl.BlockSpec((None, 1, d), lambda i, j: (i // per_batch, 0, scale_blk)),
        pl.BlockSpec((d, tn), lambda i, j: (0, j)),
    ]
    out_specs = pl.BlockSpec((tm, tn), lambda i, j: (i, j))
    out_shape = jax.ShapeDtypeStruct((t, n), BF16)
    args = [x2, g, mod3, mod3, w]
    if wg is not None:
        in_specs.append(pl.BlockSpec((d, LANES), lambda i, j: (0, 0)))
        out_specs = [out_specs, pl.BlockSpec((tm, LANES), lambda i, j: (i, 0))]
        out_shape = [out_shape, jax.ShapeDtypeStruct((t, LANES), F32)]
        args.append(wg)
    return pl.pallas_call(
        functools.partial(_nmm_kernel, with_gates=wg is not None,
                          lead_tiles=lead_cols // tn, lead_scale=lead_scale),
        grid=(t // tm, n // tn),
        in_specs=in_specs,
        out_specs=out_specs,
        out_shape=out_shape,
        scratch_shapes=[pltpu.VMEM((tm, d), BF16)],
        compiler_params=pltpu.CompilerParams(
            dimension_semantics=("arbitrary", "arbitrary"), vmem_limit_bytes=VMEM_LIMIT),
        name=name,
    )(*args)


def _mlstm_kernel(gb_ref, q_ref, k_ref, v_ref, o_ref, z_ref, gr_ref, cwq_ref, cwk_ref,
                  cbq_ref, cbk_ref, gh_ref, y_ref, c_scr, n_scr, m_scr, hq_scr, hk_scr):
    head = pl.program_id(1)
    step = pl.program_id(2)
    L = q_ref.shape[0]

    @pl.when(step == 0)
    def _():
        c_scr[...] = jnp.zeros_like(c_scr)
        n_scr[...] = jnp.zeros_like(n_scr)
        m_scr[...] = jnp.zeros_like(m_scr)
        hq_scr[...] = jnp.zeros_like(hq_scr)
        hk_scr[...] = jnp.zeros_like(hk_scr)

    row8 = lax.broadcasted_iota(jnp.int32, (SUBLANES, q_ref.shape[1]), 0)

    def conv_silu(u, halo_scr, w_ref, b_ref):
        halo = halo_scr[...]
        acc = b_ref[...]
        for j in range(CONV_W - 1):
            s = CONV_W - 1 - j
            r = pltpu.roll(u, s, 0)
            first = jnp.where(row8 < s, pltpu.roll(halo, s, 0), r[:SUBLANES])
            acc = acc + jnp.concatenate([first, r[SUBLANES:]], axis=0) * w_ref[j:j + 1, :]
        acc = acc + u * w_ref[CONV_W - 1:CONV_W, :]
        halo_scr[...] = u[L - SUBLANES:]
        return _silu(acc)

    q = conv_silu(q_ref[...].astype(F32), hq_scr, cwq_ref, cbq_ref)
    k = conv_silu(k_ref[...].astype(F32), hk_scr, cwk_ref, cbk_ref) * (A_DK ** -0.5)
    vb = v_ref[...]

    li_row = gr_ref[0] + gb_ref[head]
    lf_row = _log_sigmoid(gr_ref[1] + gb_ref[A_HEADS + head])

    row = lax.broadcasted_iota(jnp.int32, (L, L), 0)
    col = lax.broadcasted_iota(jnp.int32, (L, L), 1)
    causal = col <= row
    eye = col == row
    b_col = jnp.sum(jnp.where(causal, lf_row, 0.0), axis=1, keepdims=True)
    b_row = jnp.sum(jnp.where(eye, b_col, 0.0), axis=0, keepdims=True)
    a_row = li_row - b_row
    dm = jnp.where(causal, b_col + a_row, NEG)
    m_prev = m_scr[...]
    inter = b_col + m_prev
    m_t = jnp.maximum(inter, jnp.max(dm, axis=1, keepdims=True))
    w_intra = jnp.exp(dm - m_t)
    w_inter = jnp.exp(inter - m_t)

    qb = q.astype(BF16)
    kb = k.astype(BF16)
    s = lax.dot_general(qb, kb, (((1,), (1,)), ((), ())), preferred_element_type=F32) * w_intra
    c_old = c_scr[...]
    n_old = n_scr[...]
    num = (w_inter * jnp.dot(qb, c_old.astype(BF16), preferred_element_type=F32)
           + jnp.dot(s.astype(BF16), vb, preferred_element_type=F32))
    den = (w_inter * jnp.sum(q * n_old, axis=1, keepdims=True)
           + jnp.sum(s, axis=1, keepdims=True))
    hh = num / jnp.maximum(jnp.abs(den), jnp.exp(-m_t))

    hn = hh * lax.rsqrt(jnp.mean(hh * hh, axis=-1, keepdims=True) + EPS) * gh_ref[...]
    y = _sigmoid(o_ref[...].astype(F32)) * hn * _silu(z_ref[...].astype(F32))
    y_ref[...] = y.astype(y_ref.dtype)

    b_last = b_col[L - 1:L, :]
    g_row = b_last + a_row
    m_new = jnp.maximum(b_last + m_prev, jnp.max(g_row, axis=1, keepdims=True))
    decay = jnp.exp(b_last + m_prev - m_new)
    ws_row = jnp.exp(g_row - m_new)
    ws_col = jnp.sum(jnp.where(eye, ws_row, 0.0), axis=1, keepdims=True)
    kw = k * ws_col
    c_scr[...] = decay * c_old + lax.dot_general(
        kw.astype(BF16), vb, (((0,), (0,)), ((), ())), preferred_element_type=F32)
    n_scr[...] = decay * n_old + jnp.sum(kw, axis=0, keepdims=True)
    m_scr[...] = m_new


def _mlstm_call(proj, gates_row, gate_b, conv_w, conv_b, g_head):
    bn, seq, _ = proj.shape
    L = MLSTM_CHUNK
    kq = A_QK // A_DK
    v0 = 2 * A_QK // A_DV
    o0 = v0 + A_HEADS
    z0 = o0 + A_HEADS
    return pl.pallas_call(
        _mlstm_kernel,
        grid=(bn, A_HEADS, seq // L),
        in_specs=[
            pl.BlockSpec(memory_space=pltpu.SMEM),
            pl.BlockSpec((None, L, A_DK), lambda b, h, i: (b, i, h)),
            pl.BlockSpec((None, L, A_DK), lambda b, h, i: (b, i, kq + h)),
            pl.BlockSpec((None, L, A_DV), lambda b, h, i: (b, i, v0 + h)),
            pl.BlockSpec((None, L, A_DV), lambda b, h, i: (b, i, o0 + h)),
            pl.BlockSpec((None, L, A_DV), lambda b, h, i: (b, i, z0 + h)),
            pl.BlockSpec((None, 2, None, 1, L), lambda b, h, i: (b, 0, h, 0, i)),
            pl.BlockSpec((CONV_W, A_DK), lambda b, h, i: (0, h)),
            pl.BlockSpec((CONV_W, A_DK), lambda b, h, i: (0, kq + h)),
            pl.BlockSpec((1, A_DK), lambda b, h, i: (0, h)),
            pl.BlockSpec((1, A_DK), lambda b, h, i: (0, kq + h)),
            pl.BlockSpec((1, A_DV), lambda b, h, i: (0, h)),
        ],
        out_specs=pl.BlockSpec((None, L, A_DV), lambda b, h, i: (b, i, h)),
        out_shape=jax.ShapeDtypeStruct((bn, seq, A_INNER), BF16),
        scratch_shapes=[
            pltpu.VMEM((A_DK, A_DV), F32),
            pltpu.VMEM((1, A_DK), F32),
            pltpu.VMEM((1, 1), F32),
            pltpu.VMEM((SUBLANES, A_DK), F32),
            pltpu.VMEM((SUBLANES, A_DK), F32),
        ],
        compiler_params=pltpu.CompilerParams(
            dimension_semantics=("arbitrary", "arbitrary", "arbitrary"), vmem_limit_bytes=VMEM_LIMIT),
        name="mlstm_core",
    )(gate_b, proj, proj, proj, proj, proj, gates_row, conv_w, conv_w, conv_b, conv_b, g_head)


def _out_kernel(y_ref, w_ref, x_ref, g_ref, gate_ref, o_ref):
    y = jnp.dot(y_ref[...], w_ref[...], preferred_element_type=F32)
    yn = y * lax.rsqrt(jnp.mean(y * y, axis=-1, keepdims=True) + EPS) * g_ref[...]
    o_ref[...] = x_ref[...] + gate_ref[...] * yn


def _out_call(y2, w, x2, g, mod3, gate_blk, seq, name):
    t, kdim = y2.shape
    d = w.shape[1]
    tm = 512
    per_batch = seq // tm
    return pl.pallas_call(
        _out_kernel,
        grid=(t // tm,),
        in_specs=[
            pl.BlockSpec((tm, kdim), lambda i: (i, 0)),
            pl.BlockSpec((kdim, d), lambda i: (0, 0), pipeline_mode=pl.Buffered(1)),
            pl.BlockSpec((tm, d), lambda i: (i, 0)),
            pl.BlockSpec((1, d), lambda i: (0, 0)),
            pl.BlockSpec((None, 1, d), lambda i: (i // per_batch, 0, gate_blk)),
        ],
        out_specs=pl.BlockSpec((tm, d), lambda i: (i, 0)),
        out_shape=jax.ShapeDtypeStruct((t, d), F32),
        compiler_params=pltpu.CompilerParams(
            dimension_semantics=("arbitrary",), vmem_limit_bytes=VMEM_LIMIT),
        name=name,
    )(y2, w, x2, g, mod3)


def _bias_kernel(tbl_ref, o_ref):
    head = pl.program_id(0)
    nb, lq, _ = o_ref.shape
    width = nb * LANES + lq
    u = lax.broadcasted_iota(jnp.int32, (SUBLANES, width), 1)
    u = jnp.where(u >= nb * LANES, u - width, u)
    idx = jnp.clip(PAD - u, -MAX_REL, MAX_REL) + MAX_REL

    def body(c, base):
        return jnp.where(idx == c, tbl_ref[head, c], base)

    base = lax.fori_loop(0, 2 * MAX_REL + 1, body, jnp.zeros((SUBLANES, width), F32))
    tile = jnp.concatenate([base] * (lq // SUBLANES), axis=0)
    r = lax.broadcasted_iota(jnp.int32, (lq, width), 0)
    tile = pltpu.roll(tile, 0, 1, stride=1, stride_axis=0)
    uu = lax.broadcasted_iota(jnp.int32, (lq, width), 1)
    cq = r // CHUNK
    ck = uu // CHUNK - LEFT_CHUNKS
    visible = (ck <= cq) & (ck >= cq - LEFT_CHUNKS)
    tile = jnp.where(visible, tile * LOG2E, NEG)
    for j in range(nb):
        o_ref[j] = tile[:, j * LANES:(j + 1) * LANES]


def _bias_call(rel_table):
    return pl.pallas_call(
        _bias_kernel,
        grid=(B_HEADS,),
        in_specs=[pl.BlockSpec(memory_space=pltpu.SMEM)],
        out_specs=pl.BlockSpec((None, ATTN_NB, ATTN_LQ, LANES), lambda h: (h, 0, 0, 0)),
        out_shape=jax.ShapeDtypeStruct((B_HEADS, ATTN_NB, ATTN_LQ, LANES), F32),
        compiler_params=pltpu.CompilerParams(
            dimension_semantics=("arbitrary",), vmem_limit_bytes=VMEM_LIMIT),
        name="relpos_bias",
    )(rel_table)


def _attn_kernel(q_ref, z_ref, k_ref, v_ref, wb_ref, y_ref):
    lq = q_ref.shape[0]
    wk = PAD + lq
    start = pl.program_id(2) * lq
    wstart = pl.multiple_of(jnp.maximum(start - PAD, 0), LANES)
    off = jnp.maximum(PAD - start, 0) // LANES
    for h in range(q_ref.shape[1] // B_DH):
        cols = slice(h * B_DH, (h + 1) * B_DH)
        kwin = k_ref[pl.ds(wstart, wk), cols]
        vwin = v_ref[pl.ds(wstart, wk), cols]
        s = lax.dot_general(q_ref[:, cols], kwin, (((1,), (1,)), ((), ())), preferred_element_type=F32)
        s = s + jnp.concatenate([wb_ref[h, off + j] for j in range(wk // LANES)], axis=1)
        m = jnp.max(s, axis=1, keepdims=True)
        p = jnp.exp2(s - m)
        inv = 1.0 / jnp.sum(p, axis=1, keepdims=True)
        o = jnp.dot(p.astype(BF16), vwin, preferred_element_type=F32) * inv
        y_ref[:, cols] = (o * _silu(z_ref[:, cols].astype(F32))).astype(y_ref.dtype)


def _attn_call(qz, kv, wb):
    bn, seq, _ = qz.shape
    lq = ATTN_LQ
    gw = ATTN_HPG * B_DH
    ng = B_HEADS // ATTN_HPG
    resident = pl.Buffered(1)
    return pl.pallas_call(
        _attn_kernel,
        grid=(ng, bn, seq // lq),
        in_specs=[
            pl.BlockSpec((None, lq, gw), lambda g, b, i: (b, i, g)),
            pl.BlockSpec((None, lq, gw), lambda g, b, i: (b, i, ng + g)),
            pl.BlockSpec((None, seq, gw), lambda g, b, i: (b, 0, g), pipeline_mode=resident),
            pl.BlockSpec((None, seq, gw), lambda g, b, i: (b, 0, ng + g), pipeline_mode=resident),
            pl.BlockSpec((ATTN_HPG, ATTN_NB, lq, LANES), lambda g, b, i: (g, 0, 0, 0), pipeline_mode=resident),
        ],
        out_specs=pl.BlockSpec((None, lq, gw), lambda g, b, i: (b, i, g)),
        out_shape=jax.ShapeDtypeStruct((bn, seq, B_WIDTH), BF16),
        compiler_params=pltpu.CompilerParams(
            dimension_semantics=("arbitrary", "arbitrary", "arbitrary"), vmem_limit_bytes=VMEM_LIMIT),
        name="band_attention",
    )(qz, qz, kv, kv, wb)


def kernel(x, c, ada_w, ada_b, g_pre, g_post, a_w_in, a_conv_w, a_conv_b, a_gate_b, a_g_head, a_w_out,
           kv_ada_w, kv_ada_b, kv_g, kv_w, b_w_in, b_rel, b_w_out):
    bn, seq, d = x.shape
    t = bn * seq
    assert bn <= SUBLANES and seq % 1024 == 0 and seq >= PAD + ATTN_LQ
    x2 = x.reshape(t, d)
    c8 = jnp.pad(c.astype(F32), ((0, SUBLANES - bn), (0, 0)))

    mod0 = _mod_call(c8, ada_w, ada_b, layer=0).reshape(SUBLANES, 1, 3 * d)
    w_in = a_w_in[0]
    n_main = 2 * A_QK + 3 * A_INNER
    w_main = w_in[:, :n_main].astype(BF16)
    w_gate = jnp.pad(w_in[:, n_main:], ((0, 0), (0, LANES - 2 * A_HEADS))).astype(BF16)
    proj, gates = _nmm_call(x2, g_pre[0].reshape(1, d), mod0, 0, 1, w_main, w_gate, seq, "mlstm_in_proj")
    gates_row = gates[:, :2 * A_HEADS].reshape(bn, seq, 2, A_HEADS).transpose(0, 2, 3, 1)
    gates_row = gates_row.reshape(bn, 2, A_HEADS, 1, seq)
    y0 = _mlstm_call(proj.reshape(bn, seq, n_main), gates_row, a_gate_b[0], a_conv_w[0],
                     a_conv_b[0].reshape(1, 2 * A_QK), a_g_head[0].reshape(1, A_INNER))
    x2 = _out_call(y0.reshape(t, A_INNER), a_w_out[0].astype(BF16), x2, g_post[0].reshape(1, d),
                   mod0, 2, seq, "mlstm_out_proj")

    mod1 = _mod_call(c8, ada_w, ada_b, layer=1).reshape(SUBLANES, 1, 3 * d)
    kmod = _mod_call(c8, kv_ada_w, kv_ada_b).reshape(SUBLANES, 1, 2 * d)
    qz = _nmm_call(x2, g_pre[1].reshape(1, d), mod1, 0, 1, b_w_in[0].astype(BF16), None, seq, "attn_in_proj",
                   lead_cols=B_WIDTH, lead_scale=(B_DH ** -0.5) * LOG2E)
    kv = _nmm_call(x2, kv_g.reshape(1, d), kmod, 0, 1, kv_w.astype(BF16), None, seq, "attn_kv_proj")
    wb = _bias_call(b_rel[0])
    y1 = _attn_call(qz.reshape(bn, seq, 2 * B_WIDTH), kv.reshape(bn, seq, 2 * B_WIDTH), wb)
    x2 = _out_call(y1.reshape(t, B_WIDTH), b_w_out[0].astype(BF16), x2, g_post[1].reshape(1, d),
                   mod1, 2, seq, "attn_out_proj")
    return x2.reshape(bn, seq, d)
```

```python
import functools

import jax
import jax.numpy as jnp
from jax import lax
from jax.experimental import pallas as pl
from jax.experimental.pallas import tpu as pltpu

F32 = jnp.float32
BF16 = jnp.bfloat16

EPS = 1e-6
CHUNK = 64
A_HEADS = 8
A_DK = 256
A_DV = 512
A_QK = A_HEADS * A_DK
A_INNER = A_HEADS * A_DV
CONV_W = 4
B_HEADS = 16
B_DH = 128
B_WIDTH = B_HEADS * B_DH
LEFT_CHUNKS = 8
PAD = LEFT_CHUNKS * CHUNK
MAX_REL = 128
NEG = -1e30

LANES = 128
SUBLANES = 8

LOG2E = 1.4426950408889634

MLSTM_CHUNK = 256
MLSTM_HPG = 2
ATTN_LQ = 256
ATTN_HPG = 8
ATTN_WK = PAD + ATTN_LQ
ATTN_NB = (PAD + ATTN_WK) // LANES

VMEM_LIMIT = 48 * 1024 * 1024


def _sigmoid(v):
    return 0.5 * jnp.tanh(0.5 * v) + 0.5


def _silu(v):
    hv = 0.5 * v
    return hv * jnp.tanh(hv) + hv


def _log_sigmoid(v):
    return -(jnp.maximum(-v, 0.0) + jnp.log1p(jnp.exp(-jnp.abs(v))))


def _mod_kernel(c_ref, w_ref, b_ref, o_ref):
    sc = _silu(c_ref[...])
    o_ref[...] = jnp.dot(sc.astype(BF16), w_ref[...].astype(BF16),
                         preferred_element_type=F32) + b_ref[...]


def _mod_call(c8, w, b, layer=None):
    d = c8.shape[1]
    n = w.shape[-1]
    tn = 1024
    if layer is None:
        w_spec = pl.BlockSpec((d, tn), lambda j: (0, j))
        b_spec = pl.BlockSpec((1, tn), lambda j: (0, j))
        b2 = b.reshape(1, n)
    else:
        w_spec = pl.BlockSpec((None, d, tn), lambda j: (layer, 0, j))
        b_spec = pl.BlockSpec((None, 1, tn), lambda j: (layer, 0, j))
        b2 = b.reshape(b.shape[0], 1, n)
    return pl.pallas_call(
        _mod_kernel,
        grid=(n // tn,),
        in_specs=[pl.BlockSpec((SUBLANES, d), lambda j: (0, 0)), w_spec, b_spec],
        out_specs=pl.BlockSpec((SUBLANES, tn), lambda j: (0, j)),
        out_shape=jax.ShapeDtypeStruct((SUBLANES, n), F32),
        compiler_params=pltpu.CompilerParams(
            dimension_semantics=("arbitrary",), vmem_limit_bytes=VMEM_LIMIT),
        name="adaln_mod",
    )(c8, w, b2)


def _cast_kernel(w_ref, wg_ref, o_ref, og_ref):
    o_ref[...] = w_ref[...].astype(BF16)

    @pl.when(pl.program_id(0) == 0)
    def _():
        lane = lax.broadcasted_iota(jnp.int32, wg_ref.shape, 1)
        og_ref[...] = jnp.where(lane < 2 * A_HEADS, wg_ref[...], 0.0).astype(BF16)


def _cast_w_in_call(w_in, n_main):
    _, d, _ = w_in.shape
    tn = 1024
    return pl.pallas_call(
        _cast_kernel,
        grid=(n_main // tn,),
        in_specs=[
            pl.BlockSpec((None, d, tn), lambda j: (0, 0, j)),
            pl.BlockSpec((None, d, LANES), lambda j: (0, 0, n_main // LANES)),
        ],
        out_specs=[
            pl.BlockSpec((d, tn), lambda j: (0, j)),
            pl.BlockSpec((d, LANES), lambda j: (0, 0)),
        ],
        out_shape=[jax.ShapeDtypeStruct((d, n_main), BF16), jax.ShapeDtypeStruct((d, LANES), BF16)],
        compiler_params=pltpu.CompilerParams(
            dimension_semantics=("arbitrary",), vmem_limit_bytes=VMEM_LIMIT),
        name="cast_w_in",
    )(w_in, w_in)


def _nmm_kernel(x_ref, g_ref, shift_ref, scale_ref, w_ref, *rest, with_gates, lead_tiles, lead_scale):
    if with_gates:
        wg_ref, o_ref, og_ref, h_scr = rest
    else:
        o_ref, h_scr = rest

    @pl.when(pl.program_id(1) == 0)
    def _():
        x = x_ref[...]
        ms = jnp.mean(x * x, axis=-1, keepdims=True)
        hn = x * lax.rsqrt(ms + EPS) * g_ref[...]
        hb = (hn * (1.0 + scale_ref[...]) + shift_ref[...]).astype(BF16)
        h_scr[...] = hb
        if with_gates:
            og_ref[...] = jnp.dot(hb, wg_ref[...], preferred_element_type=F32)

    acc = jnp.dot(h_scr[...], w_ref[...], preferred_element_type=F32)
    if lead_tiles:
        acc = acc * jnp.where(pl.program_id(1) < lead_tiles, lead_scale, 1.0)
    o_ref[...] = acc.astype(o_ref.dtype)


def _nmm_call(x2, g, mod3, shift_blk, scale_blk, w, wg, seq, name, lead_cols=0, lead_scale=1.0):
    t, d = x2.shape
    n = w.shape[1]
    tm, tn = 1024, 1024
    assert lead_cols % tn == 0
    per_batch = seq // tm
    in_specs = [
        pl.BlockSpec((tm, d), lambda i, j: (i, 0)),
        pl.BlockSpec((1, d), lambda i, j: (0, 0)),
        pl.BlockSpec((None, 1, d), lambda i, j: (i // per_batch, 0, shift_blk)),
        pl.BlockSpec((None, 1, d), lambda i, j: (i // per_batch, 0, scale_blk)),
        pl.BlockSpec((d, tn), lambda i, j: (0, j)),
    ]
    out_specs = pl.BlockSpec((tm, tn), lambda i, j: (i, j))
    out_shape = jax.ShapeDtypeStruct((t, n), BF16)
    args = [x2, g, mod3, mod3, w]
    if wg is not None:
        in_specs.append(pl.BlockSpec((d, LANES), lambda i, j: (0, 0)))
        out_specs = [out_specs, pl.BlockSpec((tm, LANES), lambda i, j: (i, 0))]
        out_shape = [out_shape, jax.ShapeDtypeStruct((t, LANES), F32)]
        args.append(wg)
    return pl.pallas_call(
        functools.partial(_nmm_kernel, with_gates=wg is not None,
                          lead_tiles=lead_cols // tn, lead_scale=lead_scale),
        grid=(t // tm, n // tn),
        in_specs=in_specs,
        out_specs=out_specs,
        out_shape=out_shape,
        scratch_shapes=[pltpu.VMEM((tm, d), BF16)],
        compiler_params=pltpu.CompilerParams(
            dimension_semantics=("arbitrary", "arbitrary"), vmem_limit_bytes=VMEM_LIMIT),
        name=name,
    )(*args)


def _mlstm_kernel(gb_ref, sh_ref, q_ref, k_ref, v_ref, o_ref, z_ref, gr_ref, cwq_ref, cwk_ref,
                  cbq_ref, cbk_ref, gh_ref, y_ref, c_scr, n_scr, m_scr, hq_scr, hk_scr):
    group = pl.program_id(1)
    L = q_ref.shape[0]
    hp = q_ref.shape[1] // A_DK

    @pl.when(pl.program_id(2) == 0)
    def _():
        c_scr[...] = jnp.zeros_like(c_scr)
        n_scr[...] = jnp.zeros_like(n_scr)
        m_scr[...] = jnp.zeros_like(m_scr)
        hq_scr[...] = jnp.zeros_like(hq_scr)
        hk_scr[...] = jnp.zeros_like(hk_scr)

    row8 = lax.broadcasted_iota(jnp.int32, (SUBLANES, q_ref.shape[1]), 0)

    def conv_silu(u_ref, halo_scr, w_ref, b_ref):
        ub = u_ref[...]
        u = ub.astype(F32)
        shifted = jnp.dot(sh_ref[...], ub, preferred_element_type=F32)
        halo = halo_scr[...]
        acc = b_ref[...]
        for j in range(CONV_W - 1):
            s = CONV_W - 1 - j
            blk = shifted[(s - 1) * L:s * L]
            fix = jnp.where(row8 < s, pltpu.roll(halo, s, 0), 0.0)
            blk = jnp.concatenate([blk[:SUBLANES] + fix, blk[SUBLANES:]], axis=0)
            acc = acc + blk * w_ref[j:j + 1, :]
        acc = acc + u * w_ref[CONV_W - 1:CONV_W, :]
        halo_scr[...] = u[L - SUBLANES:]
        return _silu(acc)

    q_all = conv_silu(q_ref, hq_scr, cwq_ref, cbq_ref)
    k_all = conv_silu(k_ref, hk_scr, cwk_ref, cbk_ref) * (A_DK ** -0.5)

    row = lax.broadcasted_iota(jnp.int32, (L, L), 0)
    col = lax.broadcasted_iota(jnp.int32, (L, L), 1)
    causal = col <= row
    eye = col == row

    for h in range(hp):
        head = group * hp + h
        q = q_all[:, h * A_DK:(h + 1) * A_DK]
        k = k_all[:, h * A_DK:(h + 1) * A_DK]
        vcols = slice(h * A_DV, (h + 1) * A_DV)
        vb = v_ref[:, vcols]

        li_row = gr_ref[0, h] + gb_ref[head]
        lf_row = _log_sigmoid(gr_ref[1, h] + gb_ref[A_HEADS + head])

        b_col = jnp.sum(jnp.where(causal, lf_row, 0.0), axis=1, keepdims=True)
        b_row = jnp.sum(jnp.where(eye, b_col, 0.0), axis=0, keepdims=True)
        a_row = li_row - b_row
        dm = jnp.where(causal, b_col + a_row, NEG)
        m_prev = m_scr[h]
        inter = b_col + m_prev
        m_t = jnp.maximum(inter, jnp.max(dm, axis=1, keepdims=True))
        w_intra = jnp.exp(dm - m_t)
        w_inter = jnp.exp(inter - m_t)

        qb = q.astype(BF16)
        kb = k.astype(BF16)
        s = lax.dot_general(qb, kb, (((1,), (1,)), ((), ())), preferred_element_type=F32) * w_intra
        c_old = c_scr[h]
        n_old = n_scr[h]
        num = (w_inter * jnp.dot(qb, c_old.astype(BF16), preferred_element_type=F32)
               + jnp.dot(s.astype(BF16), vb, preferred_element_type=F32))
        den = (w_inter * jnp.sum(q * n_old, axis=1, keepdims=True)
               + jnp.sum(s, axis=1, keepdims=True))
        r = 1.0 / jnp.maximum(jnp.abs(den), jnp.exp(-m_t))
        rn = r * lax.rsqrt(r * r * jnp.mean(num * num, axis=-1, keepdims=True) + EPS)
        hn = num * rn * gh_ref[:, vcols]
        y = _sigmoid(o_ref[:, vcols].astype(F32)) * hn * _silu(z_ref[:, vcols].astype(F32))
        y_ref[:, vcols] = y.astype(y_ref.dtype)

        b_last = b_col[L - 1:L, :]
        g_row = b_last + a_row
        m_new = jnp.maximum(b_last + m_prev, jnp.max(g_row, axis=1, keepdims=True))
        decay = jnp.exp(b_last + m_prev - m_new)
        ws_row = jnp.exp(g_row - m_new)
        ws_col = jnp.sum(jnp.where(eye, ws_row, 0.0), axis=1, keepdims=True)
        kw = k * ws_col
        c_scr[h] = decay * c_old + lax.dot_general(
            kw.astype(BF16), vb, (((0,), (0,)), ((), ())), preferred_element_type=F32)
        n_scr[h] = decay * n_old + jnp.sum(kw, axis=0, keepdims=True)
        m_scr[h] = m_new


def _shift_matrix(L):
    t = jnp.arange(L)
    blocks = [(t[:, None] - s == t[None, :]) for s in range(1, CONV_W)]
    return jnp.concatenate(blocks, axis=0).astype(BF16)


def _mlstm_call(proj, gates_row, gate_b, conv_w, conv_b, g_head):
    bn, seq, _ = proj.shape
    L = MLSTM_CHUNK
    hp = MLSTM_HPG
    ng = A_HEADS // hp
    qw, vw = hp * A_DK, hp * A_DV
    v0 = 2 * A_QK // vw
    return pl.pallas_call(
        _mlstm_kernel,
        grid=(bn, ng, seq // L),
        in_specs=[
            pl.BlockSpec(memory_space=pltpu.SMEM),
            pl.BlockSpec(((CONV_W - 1) * L, L), lambda b, g, i: (0, 0)),
            pl.BlockSpec((None, L, qw), lambda b, g, i: (b, i, g)),
            pl.BlockSpec((None, L, qw), lambda b, g, i: (b, i, ng + g)),
            pl.BlockSpec((None, L, vw), lambda b, g, i: (b, i, v0 + g)),
            pl.BlockSpec((None, L, vw), lambda b, g, i: (b, i, v0 + ng + g)),
            pl.BlockSpec((None, L, vw), lambda b, g, i: (b, i, v0 + 2 * ng + g)),
            pl.BlockSpec((None, 2, hp, 1, L), lambda b, g, i: (b, 0, g, 0, i)),
            pl.BlockSpec((CONV_W, qw), lambda b, g, i: (0, g)),
            pl.BlockSpec((CONV_W, qw), lambda b, g, i: (0, ng + g)),
            pl.BlockSpec((1, qw), lambda b, g, i: (0, g)),
            pl.BlockSpec((1, qw), lambda b, g, i: (0, ng + g)),
            pl.BlockSpec((1, vw), lambda b, g, i: (0, g)),
        ],
        out_specs=pl.BlockSpec((None, L, vw), lambda b, g, i: (b, i, g)),
        out_shape=jax.ShapeDtypeStruct((bn, seq, A_INNER), BF16),
        scratch_shapes=[
            pltpu.VMEM((hp, A_DK, A_DV), F32),
            pltpu.VMEM((hp, 1, A_DK), F32),
            pltpu.VMEM((hp, 1, 1), F32),
            pltpu.VMEM((SUBLANES, qw), F32),
            pltpu.VMEM((SUBLANES, qw), F32),
        ],
        compiler_params=pltpu.CompilerParams(
            dimension_semantics=("arbitrary", "arbitrary", "arbitrary"), vmem_limit_bytes=VMEM_LIMIT),
        name="mlstm_core",
    )(gate_b, _shift_matrix(L), proj, proj, proj, proj, proj, gates_row, conv_w, conv_w, conv_b, conv_b, g_head)


def _out_kernel(y_ref, w_ref, x_ref, g_ref, gate_ref, o_ref):
    y = jnp.dot(y_ref[...], w_ref[...], preferred_element_type=F32)
    yn = y * lax.rsqrt(jnp.mean(y * y, axis=-1, keepdims=True) + EPS) * g_ref[...]
    o_ref[...] = x_ref[...] + gate_ref[...] * yn


def _out_call(y2, w, x2, g, mod3, gate_blk, seq, name):
    t, kdim = y2.shape
    d = w.shape[1]
    tm = 512
    per_batch = seq // tm
    return pl.pallas_call(
        _out_kernel,
        grid=(t // tm,),
        in_specs=[
            pl.BlockSpec((tm, kdim), lambda i: (i, 0)),
            pl.BlockSpec((kdim, d), lambda i: (0, 0), pipeline_mode=pl.Buffered(1)),
            pl.BlockSpec((tm, d), lambda i: (i, 0)),
            pl.BlockSpec((1, d), lambda i: (0, 0)),
            pl.BlockSpec((None, 1, d), lambda i: (i // per_batch, 0, gate_blk)),
        ],
        out_specs=pl.BlockSpec((tm, d), lambda i: (i, 0)),
        out_shape=jax.ShapeDtypeStruct((t, d), F32),
        compiler_params=pltpu.CompilerParams(
            dimension_semantics=("arbitrary",), vmem_limit_bytes=VMEM_LIMIT),
        name=name,
    )(y2, w, x2, g, mod3)


def _bias_kernel(tbl_ref, o_ref):
    head = pl.program_id(0)
    nb, lq, _ = o_ref.shape
    lo = PAD - MAX_REL
    hi = lo + 3 * LANES
    um = lo + lax.broadcasted_iota(jnp.int32, (SUBLANES, hi - lo), 1)
    idx = jnp.clip(PAD - um, -MAX_REL, MAX_REL) + MAX_REL

    def body(c, mid):
        return jnp.where(idx == c, tbl_ref[head, c], mid)

    mid = lax.fori_loop(0, 2 * MAX_REL + 1, body, jnp.zeros((SUBLANES, hi - lo), F32))
    far_left = tbl_ref[head, 2 * MAX_REL]
    far_right = tbl_ref[head, 0]
    base = jnp.concatenate([
        jnp.full((SUBLANES, lo), far_left, F32), mid,
        jnp.full((SUBLANES, nb * LANES - hi), far_right, F32),
        jnp.full((SUBLANES, lq), far_left, F32),
    ], axis=1)
    width = nb * LANES + lq
    tile = jnp.concatenate([base] * (lq // SUBLANES), axis=0)
    r = lax.broadcasted_iota(jnp.int32, (lq, width), 0)
    tile = pltpu.roll(tile, 0, 1, stride=1, stride_axis=0)
    uu = lax.broadcasted_iota(jnp.int32, (lq, width), 1)
    cq = r // CHUNK
    ck = uu // CHUNK - LEFT_CHUNKS
    visible = (ck <= cq) & (ck >= cq - LEFT_CHUNKS)
    tile = jnp.where(visible, tile * LOG2E, NEG)
    for j in range(nb):
        o_ref[j] = tile[:, j * LANES:(j + 1) * LANES]


def _bias_call(rel_table):
    return pl.pallas_call(
        _bias_kernel,
        grid=(B_HEADS,),
        in_specs=[pl.BlockSpec(memory_space=pltpu.SMEM)],
        out_specs=pl.BlockSpec((None, ATTN_NB, ATTN_LQ, LANES), lambda h: (h, 0, 0, 0)),
        out_shape=jax.ShapeDtypeStruct((B_HEADS, ATTN_NB, ATTN_LQ, LANES), F32),
        compiler_params=pltpu.CompilerParams(
            dimension_semantics=("arbitrary",), vmem_limit_bytes=VMEM_LIMIT),
        name="relpos_bias",
    )(rel_table)


def _attn_kernel(q_ref, z_ref, k_ref, v_ref, wb_ref, y_ref):
    lq = q_ref.shape[0]
    wk = PAD + lq
    start = pl.program_id(2) * lq
    wstart = pl.multiple_of(jnp.maximum(start - PAD, 0), LANES)
    off = jnp.maximum(PAD - start, 0) // LANES
    for h in range(q_ref.shape[1] // B_DH):
        cols = slice(h * B_DH, (h + 1) * B_DH)
        kwin = k_ref[pl.ds(wstart, wk), cols]
        vwin = v_ref[pl.ds(wstart, wk), cols]
        s = lax.dot_general(q_ref[:, cols], kwin, (((1,), (1,)), ((), ())), preferred_element_type=F32)
        s = s + jnp.concatenate([wb_ref[h, off + j] for j in range(wk // LANES)], axis=1)
        m = jnp.max(s, axis=1, keepdims=True)
        p = jnp.exp2(s - m)
        inv = 1.0 / jnp.sum(p, axis=1, keepdims=True)
        o = jnp.dot(p.astype(BF16), vwin, preferred_element_type=F32) * inv
        y_ref[:, cols] = (o * _silu(z_ref[:, cols].astype(F32))).astype(y_ref.dtype)


def _attn_call(qz, kv, wb):
    bn, seq, _ = qz.shape
    lq = ATTN_LQ
    gw = ATTN_HPG * B_DH
    ng = B_HEADS // ATTN_HPG
    resident = pl.Buffered(1)
    return pl.pallas_call(
        _attn_kernel,
        grid=(ng, bn, seq // lq),
        in_specs=[
            pl.BlockSpec((None, lq, gw), lambda g, b, i: (b, i, g)),
            pl.BlockSpec((None, lq, gw), lambda g, b, i: (b, i, ng + g)),
            pl.BlockSpec((None, seq, gw), lambda g, b, i: (b, 0, g), pipeline_mode=resident),
            pl.BlockSpec((None, seq, gw), lambda g, b, i: (b, 0, ng + g), pipeline_mode=resident),
            pl.BlockSpec((ATTN_HPG, ATTN_NB, lq, LANES), lambda g, b, i: (g, 0, 0, 0), pipeline_mode=resident),
        ],
        out_specs=pl.BlockSpec((None, lq, gw), lambda g, b, i: (b, i, g)),
        out_shape=jax.ShapeDtypeStruct((bn, seq, B_WIDTH), BF16),
        compiler_params=pltpu.CompilerParams(
            dimension_semantics=("arbitrary", "arbitrary", "arbitrary"), vmem_limit_bytes=VMEM_LIMIT),
        name="band_attention",
    )(qz, qz, kv, kv, wb)


def kernel(x, c, ada_w, ada_b, g_pre, g_post, a_w_in, a_conv_w, a_conv_b, a_gate_b, a_g_head, a_w_out,
           kv_ada_w, kv_ada_b, kv_g, kv_w, b_w_in, b_rel, b_w_out):
    bn, seq, d = x.shape
    t = bn * seq
    assert bn <= SUBLANES and seq % 1024 == 0 and seq >= PAD + ATTN_LQ
    x2 = x.reshape(t, d)
    c8 = jnp.pad(c.astype(F32), ((0, SUBLANES - bn), (0, 0)))

    mod0 = _mod_call(c8, ada_w, ada_b, layer=0).reshape(SUBLANES, 1, 3 * d)
    n_main = 2 * A_QK + 3 * A_INNER
    w_main, w_gate = _cast_w_in_call(a_w_in, n_main)
    proj, gates = _nmm_call(x2, g_pre[0].reshape(1, d), mod0, 0, 1, w_main, w_gate, seq, "mlstm_in_proj")
    gates_row = gates[:, :2 * A_HEADS].reshape(bn, seq, 2, A_HEADS).transpose(0, 2, 3, 1)
    gates_row = gates_row.reshape(bn, 2, A_HEADS, 1, seq)
    y0 = _mlstm_call(proj.reshape(bn, seq, n_main), gates_row, a_gate_b[0], a_conv_w[0],
                     a_conv_b[0].reshape(1, 2 * A_QK), a_g_head[0].reshape(1, A_INNER))
    x2 = _out_call(y0.reshape(t, A_INNER), a_w_out[0].astype(BF16), x2, g_post[0].reshape(1, d),
                   mod0, 2, seq, "mlstm_out_proj")

    mod1 = _mod_call(c8, ada_w, ada_b, layer=1).reshape(SUBLANES, 1, 3 * d)
    kmod = _mod_call(c8, kv_ada_w, kv_ada_b).reshape(SUBLANES, 1, 2 * d)
    qz = _nmm_call(x2, g_pre[1].reshape(1, d), mod1, 0, 1, b_w_in[0].astype(BF16), None, seq, "attn_in_proj",
                   lead_cols=B_WIDTH, lead_scale=(B_DH ** -0.5) * LOG2E)
    kv = _nmm_call(x2, kv_g.reshape(1, d), kmod, 0, 1, kv_w.astype(BF16), None, seq, "attn_kv_proj")
    wb = _bias_call(b_rel[0])
    y1 = _attn_call(qz.reshape(bn, seq, 2 * B_WIDTH), kv.reshape(bn, seq, 2 * B_WIDTH), wb)
    x2 = _out_call(y1.reshape(t, B_WIDTH), b_w_out[0].astype(BF16), x2, g_post[1].reshape(1, d),
                   mod1, 2, seq, "attn_out_proj")
    return x2.reshape(bn, seq, d)
```

```python
import functools

import jax
import jax.numpy as jnp
from jax import lax
from jax.experimental import pallas as pl
from jax.experimental.pallas import tpu as pltpu

F32 = jnp.float32
BF16 = jnp.bfloat16

EPS = 1e-6
CHUNK = 64
A_HEADS = 8
A_DK = 256
A_DV = 512
A_QK = A_HEADS * A_DK
A_INNER = A_HEADS * A_DV
CONV_W = 4
B_HEADS = 16
B_DH = 128
B_WIDTH = B_HEADS * B_DH
LEFT_CHUNKS = 8
PAD = LEFT_CHUNKS * CHUNK
MAX_REL = 128
NEG = -1e30

LANES = 128
SUBLANES = 8

LOG2E = 1.4426950408889634

MLSTM_CHUNK = 256
MLSTM_HPG = 2
ATTN_LQ = 256
ATTN_HPG = 8
ATTN_WK = PAD + ATTN_LQ
ATTN_NB = (PAD + ATTN_WK) // LANES

VMEM_LIMIT = 56 * 1024 * 1024


def _sigmoid(v):
    return 0.5 * jnp.tanh(0.5 * v) + 0.5


def _silu(v):
    hv = 0.5 * v
    return hv * jnp.tanh(hv) + hv


def _log_sigmoid(v):
    return -(jnp.maximum(-v, 0.0) + jnp.log1p(jnp.exp(-jnp.abs(v))))


def _mod_kernel(c_ref, w_ref, b_ref, o_ref):
    sc = _silu(c_ref[...])
    o_ref[...] = jnp.dot(sc.astype(BF16), w_ref[...].astype(BF16),
                         preferred_element_type=F32) + b_ref[...]


def _mod_call(c8, w, b, layer=None):
    d = c8.shape[1]
    n = w.shape[-1]
    tn = 1024
    if layer is None:
        w_spec = pl.BlockSpec((d, tn), lambda j: (0, j))
        b_spec = pl.BlockSpec((1, tn), lambda j: (0, j))
        b2 = b.reshape(1, n)
    else:
        w_spec = pl.BlockSpec((None, d, tn), lambda j: (layer, 0, j))
        b_spec = pl.BlockSpec((None, 1, tn), lambda j: (layer, 0, j))
        b2 = b.reshape(b.shape[0], 1, n)
    return pl.pallas_call(
        _mod_kernel,
        grid=(n // tn,),
        in_specs=[pl.BlockSpec((SUBLANES, d), lambda j: (0, 0)), w_spec, b_spec],
        out_specs=pl.BlockSpec((SUBLANES, tn), lambda j: (0, j)),
        out_shape=jax.ShapeDtypeStruct((SUBLANES, n), F32),
        compiler_params=pltpu.CompilerParams(
            dimension_semantics=("arbitrary",), vmem_limit_bytes=VMEM_LIMIT),
        name="adaln_mod",
    )(c8, w, b2)


def _cast_kernel(w_ref, wg_ref, o_ref, og_ref):
    o_ref[...] = w_ref[...].astype(BF16)

    @pl.when(pl.program_id(0) == 0)
    def _():
        r = lax.broadcasted_iota(jnp.int32, wg_ref.shape, 0)
        og_ref[...] = jnp.where(r < 2 * A_HEADS, wg_ref[...], 0.0).astype(BF16)


def _cast_w_in_call(w_t, n_main):
    _, d = w_t.shape
    tn = 1024
    return pl.pallas_call(
        _cast_kernel,
        grid=(n_main // tn,),
        in_specs=[
            pl.BlockSpec((tn, d), lambda j: (j, 0)),
            pl.BlockSpec((LANES, d), lambda j: (n_main // LANES, 0)),
        ],
        out_specs=[
            pl.BlockSpec((tn, d), lambda j: (j, 0)),
            pl.BlockSpec((LANES, d), lambda j: (0, 0)),
        ],
        out_shape=[jax.ShapeDtypeStruct((n_main, d), BF16), jax.ShapeDtypeStruct((LANES, d), BF16)],
        compiler_params=pltpu.CompilerParams(
            dimension_semantics=("arbitrary",), vmem_limit_bytes=VMEM_LIMIT),
        name="cast_w_in",
    )(w_t, w_t)


_NT = (((1,), (1,)), ((), ()))


def _modulated(xn, g_ref, shift_ref, scale_ref):
    return ((xn * g_ref[...]) * (1.0 + scale_ref[...]) + shift_ref[...]).astype(BF16)


def _nmm_kernel(x_ref, g_ref, shift_ref, scale_ref, wt_ref, wgt_ref, o_ref, og_ref, h_scr):
    @pl.when(pl.program_id(1) == 0)
    def _():
        x = x_ref[...]
        xn = x * lax.rsqrt(jnp.mean(x * x, axis=-1, keepdims=True) + EPS)
        hb = _modulated(xn, g_ref, shift_ref, scale_ref)
        h_scr[...] = hb
        og_ref[...] = lax.dot_general(hb, wgt_ref[...], _NT, preferred_element_type=F32)

    o_ref[...] = lax.dot_general(h_scr[...], wt_ref[...], _NT, preferred_element_type=F32).astype(o_ref.dtype)


def _nmm_call(x2, g, mod3, w_t, wg_t, seq):
    t, d = x2.shape
    n = w_t.shape[0]
    tm, tn = 1024, 2048
    per_batch = seq // tm
    return pl.pallas_call(
        _nmm_kernel,
        grid=(t // tm, n // tn),
        in_specs=[
            pl.BlockSpec((tm, d), lambda i, j: (i, 0)),
            pl.BlockSpec((1, d), lambda i, j: (0, 0)),
            pl.BlockSpec((None, 1, d), lambda i, j: (i // per_batch, 0, 0)),
            pl.BlockSpec((None, 1, d), lambda i, j: (i // per_batch, 0, 1)),
            pl.BlockSpec((tn, d), lambda i, j: (j, 0)),
            pl.BlockSpec((LANES, d), lambda i, j: (0, 0)),
        ],
        out_specs=[pl.BlockSpec((tm, tn), lambda i, j: (i, j)),
                   pl.BlockSpec((tm, LANES), lambda i, j: (i, 0))],
        out_shape=[jax.ShapeDtypeStruct((t, n), BF16), jax.ShapeDtypeStruct((t, LANES), F32)],
        scratch_shapes=[pltpu.VMEM((tm, d), BF16)],
        compiler_params=pltpu.CompilerParams(
            dimension_semantics=("arbitrary", "arbitrary"), vmem_limit_bytes=VMEM_LIMIT),
        name="mlstm_in_proj",
    )(x2, g, mod3, mod3, w_t, wg_t)


def _nmm2_kernel(x_ref, g1_ref, shift1_ref, scale1_ref, g2_ref, shift2_ref, scale2_ref, w_ref,
                 o_ref, h1_scr, h2_scr, *, n1_tiles, lead_tiles, lead_scale):
    j = pl.program_id(1)

    @pl.when(j == 0)
    def _():
        x = x_ref[...]
        xn = x * lax.rsqrt(jnp.mean(x * x, axis=-1, keepdims=True) + EPS)
        h1_scr[...] = _modulated(xn, g1_ref, shift1_ref, scale1_ref)
        h2_scr[...] = _modulated(xn, g2_ref, shift2_ref, scale2_ref)

    @pl.when(j < n1_tiles)
    def _():
        acc = jnp.dot(h1_scr[...], w_ref[...], preferred_element_type=F32)
        acc = acc * jnp.where(j < lead_tiles, lead_scale, 1.0)
        o_ref[...] = acc.astype(o_ref.dtype)

    @pl.when(j >= n1_tiles)
    def _():
        o_ref[...] = jnp.dot(h2_scr[...], w_ref[...], preferred_element_type=F32).astype(o_ref.dtype)


def _nmm2_call(x2, g1, mod1, g2, mod2, w, n1, seq, lead_cols, lead_scale):
    t, d = x2.shape
    n = w.shape[1]
    tm, tn = 1024, 1024
    assert lead_cols % tn == 0 and n1 % tn == 0 and n % tn == 0
    per_batch = seq // tm
    n1_tiles = n1 // tn
    vec = lambda blk: pl.BlockSpec((None, 1, d), lambda i, j: (i // per_batch, 0, blk))
    return pl.pallas_call(
        functools.partial(_nmm2_kernel, n1_tiles=n1_tiles, lead_tiles=lead_cols // tn, lead_scale=lead_scale),
        grid=(t // tm, n // tn),
        in_specs=[
            pl.BlockSpec((tm, d), lambda i, j: (i, 0)),
            pl.BlockSpec((1, d), lambda i, j: (0, 0)), vec(0), vec(1),
            pl.BlockSpec((1, d), lambda i, j: (0, 0)), vec(0), vec(1),
            pl.BlockSpec((d, tn), lambda i, j: (0, j)),
        ],
        out_specs=pl.BlockSpec((tm, tn), lambda i, j: (i, j)),
        out_shape=jax.ShapeDtypeStruct((t, n), BF16),
        scratch_shapes=[pltpu.VMEM((tm, d), BF16), pltpu.VMEM((tm, d), BF16)],
        compiler_params=pltpu.CompilerParams(
            dimension_semantics=("arbitrary", "arbitrary"), vmem_limit_bytes=VMEM_LIMIT),
        name="attn_qzkv_proj",
    )(x2, g1, mod1, mod1, g2, mod2, mod2, w)


def _mlstm_kernel(gb_ref, sh_ref, q_ref, k_ref, v_ref, o_ref, z_ref, gr_ref, cwq_ref, cwk_ref,
                  cbq_ref, cbk_ref, gh_ref, y_ref, c_scr, n_scr, m_scr, hq_scr, hk_scr):
    group = pl.program_id(1)
    L = q_ref.shape[0]
    hp = q_ref.shape[1] // A_DK

    @pl.when(pl.program_id(2) == 0)
    def _():
        c_scr[...] = jnp.zeros_like(c_scr)
        n_scr[...] = jnp.zeros_like(n_scr)
        m_scr[...] = jnp.zeros_like(m_scr)
        hq_scr[...] = jnp.zeros_like(hq_scr)
        hk_scr[...] = jnp.zeros_like(hk_scr)

    row8 = lax.broadcasted_iota(jnp.int32, (SUBLANES, q_ref.shape[1]), 0)

    def conv_silu(u_ref, halo_scr, w_ref, b_ref):
        ub = u_ref[...]
        u = ub.astype(F32)
        shifted = jnp.dot(sh_ref[...], ub, preferred_element_type=F32)
        halo = halo_scr[...]
        acc = b_ref[...]
        for j in range(CONV_W - 1):
            s = CONV_W - 1 - j
            blk = shifted[(s - 1) * L:s * L]
            fix = jnp.where(row8 < s, pltpu.roll(halo, s, 0), 0.0)
            blk = jnp.concatenate([blk[:SUBLANES] + fix, blk[SUBLANES:]], axis=0)
            acc = acc + blk * w_ref[j:j + 1, :]
        acc = acc + u * w_ref[CONV_W - 1:CONV_W, :]
        halo_scr[...] = u[L - SUBLANES:]
        return _silu(acc)

    q_all = conv_silu(q_ref, hq_scr, cwq_ref, cbq_ref)
    k_all = conv_silu(k_ref, hk_scr, cwk_ref, cbk_ref) * (A_DK ** -0.5)

    row = lax.broadcasted_iota(jnp.int32, (L, L), 0)
    col = lax.broadcasted_iota(jnp.int32, (L, L), 1)
    causal = col <= row
    eye = col == row

    for h in range(hp):
        head = group * hp + h
        q = q_all[:, h * A_DK:(h + 1) * A_DK]
        k = k_all[:, h * A_DK:(h + 1) * A_DK]
        vcols = slice(h * A_DV, (h + 1) * A_DV)
        vb = v_ref[:, vcols]

        li_row = gr_ref[0, h] + gb_ref[head]
        lf_row = _log_sigmoid(gr_ref[1, h] + gb_ref[A_HEADS + head])

        b_col = jnp.sum(jnp.where(causal, lf_row, 0.0), axis=1, keepdims=True)
        b_row = jnp.sum(jnp.where(eye, b_col, 0.0), axis=0, keepdims=True)
        a_row = li_row - b_row
        dm = jnp.where(causal, b_col + a_row, NEG)
        m_prev = m_scr[h]
        inter = b_col + m_prev
        m_t = jnp.maximum(inter, jnp.max(dm, axis=1, keepdims=True))
        w_intra = jnp.exp(dm - m_t)
        w_inter = jnp.exp(inter - m_t)

        qb = q.astype(BF16)
        kb = k.astype(BF16)
        s = lax.dot_general(qb, kb, (((1,), (1,)), ((), ())), preferred_element_type=F32) * w_intra
        c_old = c_scr[h]
        n_old = n_scr[h]
        num = (w_inter * jnp.dot(qb, c_old.astype(BF16), preferred_element_type=F32)
               + jnp.dot(s.astype(BF16), vb, preferred_element_type=F32))
        den = (w_inter * jnp.sum(q * n_old, axis=1, keepdims=True)
               + jnp.sum(s, axis=1, keepdims=True))
        r = 1.0 / jnp.maximum(jnp.abs(den), jnp.exp(-m_t))
        rn = r * lax.rsqrt(r * r * jnp.mean(num * num, axis=-1, keepdims=True) + EPS)
        hn = num * rn * gh_ref[:, vcols]
        y = _sigmoid(o_ref[:, vcols].astype(F32)) * hn * _silu(z_ref[:, vcols].astype(F32))
        y_ref[:, vcols] = y.astype(y_ref.dtype)

        b_last = b_col[L - 1:L, :]
        g_row = b_last + a_row
        m_new = jnp.maximum(b_last + m_prev, jnp.max(g_row, axis=1, keepdims=True))
        decay = jnp.exp(b_last + m_prev - m_new)
        ws_row = jnp.exp(g_row - m_new)
        ws_col = jnp.sum(jnp.where(eye, ws_row, 0.0), axis=1, keepdims=True)
        kw = k * ws_col
        c_scr[h] = decay * c_old + lax.dot_general(
            kw.astype(BF16), vb, (((0,), (0,)), ((), ())), preferred_element_type=F32)
        n_scr[h] = decay * n_old + jnp.sum(kw, axis=0, keepdims=True)
        m_scr[h] = m_new


def _shift_matrix(L):
    t = jnp.arange(L)
    blocks = [(t[:, None] - s == t[None, :]) for s in range(1, CONV_W)]
    return jnp.concatenate(blocks, axis=0).astype(BF16)


def _mlstm_call(proj, gates_row, gate_b, conv_w, conv_b, g_head):
    bn, seq, _ = proj.shape
    L = MLSTM_CHUNK
    hp = MLSTM_HPG
    ng = A_HEADS // hp
    qw, vw = hp * A_DK, hp * A_DV
    v0 = 2 * A_QK // vw
    return pl.pallas_call(
        _mlstm_kernel,
        grid=(bn, ng, seq // L),
        in_specs=[
            pl.BlockSpec(memory_space=pltpu.SMEM),
            pl.BlockSpec(((CONV_W - 1) * L, L), lambda b, g, i: (0, 0)),
            pl.BlockSpec((None, L, qw), lambda b, g, i: (b, i, g)),
            pl.BlockSpec((None, L, qw), lambda b, g, i: (b, i, ng + g)),
            pl.BlockSpec((None, L, vw), lambda b, g, i: (b, i, v0 + g)),
            pl.BlockSpec((None, L, vw), lambda b, g, i: (b, i, v0 + ng + g)),
            pl.BlockSpec((None, L, vw), lambda b, g, i: (b, i, v0 + 2 * ng + g)),
            pl.BlockSpec((None, 2, hp, 1, L), lambda b, g, i: (b, 0, g, 0, i)),
            pl.BlockSpec((CONV_W, qw), lambda b, g, i: (0, g)),
            pl.BlockSpec((CONV_W, qw), lambda b, g, i: (0, ng + g)),
            pl.BlockSpec((1, qw), lambda b, g, i: (0, g)),
            pl.BlockSpec((1, qw), lambda b, g, i: (0, ng + g)),
            pl.BlockSpec((1, vw), lambda b, g, i: (0, g)),
        ],
        out_specs=pl.BlockSpec((None, L, vw), lambda b, g, i: (b, i, g)),
        out_shape=jax.ShapeDtypeStruct((bn, seq, A_INNER), BF16),
        scratch_shapes=[
            pltpu.VMEM((hp, A_DK, A_DV), F32),
            pltpu.VMEM((hp, 1, A_DK), F32),
            pltpu.VMEM((hp, 1, 1), F32),
            pltpu.VMEM((SUBLANES, qw), F32),
            pltpu.VMEM((SUBLANES, qw), F32),
        ],
        compiler_params=pltpu.CompilerParams(
            dimension_semantics=("arbitrary", "arbitrary", "arbitrary"), vmem_limit_bytes=VMEM_LIMIT),
        name="mlstm_core",
    )(gate_b, _shift_matrix(L), proj, proj, proj, proj, proj, gates_row, conv_w, conv_w, conv_b, conv_b, g_head)


def _out_kernel(y_ref, w_ref, x_ref, g_ref, gate_ref, o_ref):
    y = jnp.dot(y_ref[...], w_ref[...], preferred_element_type=F32)
    yn = y * lax.rsqrt(jnp.mean(y * y, axis=-1, keepdims=True) + EPS) * g_ref[...]
    o_ref[...] = x_ref[...] + gate_ref[...] * yn


def _out_call(y2, w, x2, g, mod3, gate_blk, seq, name):
    t, kdim = y2.shape
    d = w.shape[1]
    tm = 512
    per_batch = seq // tm
    return pl.pallas_call(
        _out_kernel,
        grid=(t // tm,),
        in_specs=[
            pl.BlockSpec((tm, kdim), lambda i: (i, 0)),
            pl.BlockSpec((kdim, d), lambda i: (0, 0), pipeline_mode=pl.Buffered(1)),
            pl.BlockSpec((tm, d), lambda i: (i, 0)),
            pl.BlockSpec((1, d), lambda i: (0, 0)),
            pl.BlockSpec((None, 1, d), lambda i: (i // per_batch, 0, gate_blk)),
        ],
        out_specs=pl.BlockSpec((tm, d), lambda i: (i, 0)),
        out_shape=jax.ShapeDtypeStruct((t, d), F32),
        compiler_params=pltpu.CompilerParams(
            dimension_semantics=("arbitrary",), vmem_limit_bytes=VMEM_LIMIT),
        name=name,
    )(y2, w, x2, g, mod3)


def _bias_kernel(tbl_ref, o_ref):
    head = pl.program_id(0)
    nb, lq, _ = o_ref.shape
    lo = PAD - MAX_REL
    hi = lo + 3 * LANES
    um = lo + lax.broadcasted_iota(jnp.int32, (SUBLANES, hi - lo), 1)
    idx = jnp.clip(PAD - um, -MAX_REL, MAX_REL) + MAX_REL

    def body(c, mid):
        return jnp.where(idx == c, tbl_ref[head, c], mid)

    mid = lax.fori_loop(0, 2 * MAX_REL + 1, body, jnp.zeros((SUBLANES, hi - lo), F32))
    far_left = tbl_ref[head, 2 * MAX_REL]
    far_right = tbl_ref[head, 0]
    base = jnp.concatenate([
        jnp.full((SUBLANES, lo), far_left, F32), mid,
        jnp.full((SUBLANES, nb * LANES - hi), far_right, F32),
        jnp.full((SUBLANES, lq), far_left, F32),
    ], axis=1)
    width = nb * LANES + lq
    tile = jnp.concatenate([base] * (lq // SUBLANES), axis=0)
    r = lax.broadcasted_iota(jnp.int32, (lq, width), 0)
    tile = pltpu.roll(tile, 0, 1, stride=1, stride_axis=0)
    uu = lax.broadcasted_iota(jnp.int32, (lq, width), 1)
    cq = r // CHUNK
    ck = uu // CHUNK - LEFT_CHUNKS
    visible = (ck <= cq) & (ck >= cq - LEFT_CHUNKS)
    tile = jnp.where(visible, tile * LOG2E, NEG)
    for j in range(nb):
        o_ref[j] = tile[:, j * LANES:(j + 1) * LANES]


def _bias_call(rel_table):
    return pl.pallas_call(
        _bias_kernel,
        grid=(B_HEADS,),
        in_specs=[pl.BlockSpec(memory_space=pltpu.SMEM)],
        out_specs=pl.BlockSpec((None, ATTN_NB, ATTN_LQ, LANES), lambda h: (h, 0, 0, 0)),
        out_shape=jax.ShapeDtypeStruct((B_HEADS, ATTN_NB, ATTN_LQ, LANES), F32),
        compiler_params=pltpu.CompilerParams(
            dimension_semantics=("arbitrary",), vmem_limit_bytes=VMEM_LIMIT),
        name="relpos_bias",
    )(rel_table)


def _attn_kernel(q_ref, z_ref, k_ref, v_ref, wb_ref, y_ref):
    lq = q_ref.shape[0]
    wk = PAD + lq
    start = pl.program_id(2) * lq
    wstart = pl.multiple_of(jnp.maximum(start - PAD, 0), LANES)
    off = jnp.maximum(PAD - start, 0) // LANES
    for h in range(q_ref.shape[1] // B_DH):
        cols = slice(h * B_DH, (h + 1) * B_DH)
        kwin = k_ref[pl.ds(wstart, wk), cols]
        vwin = v_ref[pl.ds(wstart, wk), cols]
        s = lax.dot_general(q_ref[:, cols], kwin, (((1,), (1,)), ((), ())), preferred_element_type=F32)
        s = s + jnp.concatenate([wb_ref[h, off + j] for j in range(wk // LANES)], axis=1)
        m = jnp.max(s, axis=1, keepdims=True)
        p = jnp.exp2(s - m)
        inv = 1.0 / jnp.sum(p, axis=1, keepdims=True)
        o = jnp.dot(p.astype(BF16), vwin, preferred_element_type=F32) * inv
        y_ref[:, cols] = (o * _silu(z_ref[:, cols].astype(F32))).astype(y_ref.dtype)


def _attn_call(qzkv, wb):
    bn, seq, _ = qzkv.shape
    lq = ATTN_LQ
    gw = ATTN_HPG * B_DH
    ng = B_HEADS // ATTN_HPG
    resident = pl.Buffered(1)
    return pl.pallas_call(
        _attn_kernel,
        grid=(ng, bn, seq // lq),
        in_specs=[
            pl.BlockSpec((None, lq, gw), lambda g, b, i: (b, i, g)),
            pl.BlockSpec((None, lq, gw), lambda g, b, i: (b, i, ng + g)),
            pl.BlockSpec((None, seq, gw), lambda g, b, i: (b, 0, 2 * ng + g), pipeline_mode=resident),
            pl.BlockSpec((None, seq, gw), lambda g, b, i: (b, 0, 3 * ng + g), pipeline_mode=resident),
            pl.BlockSpec((ATTN_HPG, ATTN_NB, lq, LANES), lambda g, b, i: (g, 0, 0, 0), pipeline_mode=resident),
        ],
        out_specs=pl.BlockSpec((None, lq, gw), lambda g, b, i: (b, i, g)),
        out_shape=jax.ShapeDtypeStruct((bn, seq, B_WIDTH), BF16),
        compiler_params=pltpu.CompilerParams(
            dimension_semantics=("arbitrary", "arbitrary", "arbitrary"), vmem_limit_bytes=VMEM_LIMIT),
        name="band_attention",
    )(qzkv, qzkv, qzkv, qzkv, wb)


def kernel(x, c, ada_w, ada_b, g_pre, g_post, a_w_in, a_conv_w, a_conv_b, a_gate_b, a_g_head, a_w_out,
           kv_ada_w, kv_ada_b, kv_g, kv_w, b_w_in, b_rel, b_w_out):
    bn, seq, d = x.shape
    t = bn * seq
    assert bn <= SUBLANES and seq % 1024 == 0 and seq >= PAD + ATTN_LQ
    x2 = x.reshape(t, d)
    c8 = jnp.pad(c.astype(F32), ((0, SUBLANES - bn), (0, 0)))

    mod0 = _mod_call(c8, ada_w, ada_b, layer=0).reshape(SUBLANES, 1, 3 * d)
    n_main = 2 * A_QK + 3 * A_INNER
    w_main_t, w_gate_t = _cast_w_in_call(a_w_in[0].T, n_main)
    proj, gates = _nmm_call(x2, g_pre[0].reshape(1, d), mod0, w_main_t, w_gate_t, seq)
    gates_row = gates[:, :2 * A_HEADS].reshape(bn, seq, 2, A_HEADS).transpose(0, 2, 3, 1)
    gates_row = gates_row.reshape(bn, 2, A_HEADS, 1, seq)
    y0 = _mlstm_call(proj.reshape(bn, seq, n_main), gates_row, a_gate_b[0], a_conv_w[0],
                     a_conv_b[0].reshape(1, 2 * A_QK), a_g_head[0].reshape(1, A_INNER))
    x2 = _out_call(y0.reshape(t, A_INNER), a_w_out[0].astype(BF16), x2, g_post[0].reshape(1, d),
                   mod0, 2, seq, "mlstm_out_proj")

    mod1 = _mod_call(c8, ada_w, ada_b, layer=1).reshape(SUBLANES, 1, 3 * d)
    kmod = _mod_call(c8, kv_ada_w, kv_ada_b).reshape(SUBLANES, 1, 2 * d)
    w_qzkv = jnp.concatenate([b_w_in[0].astype(BF16), kv_w.astype(BF16)], axis=1)
    qzkv = _nmm2_call(x2, g_pre[1].reshape(1, d), mod1, kv_g.reshape(1, d), kmod, w_qzkv, 2 * B_WIDTH, seq,
                      lead_cols=B_WIDTH, lead_scale=(B_DH ** -0.5) * LOG2E)
    wb = _bias_call(b_rel[0])
    y1 = _attn_call(qzkv.reshape(bn, seq, 4 * B_WIDTH), wb)
    x2 = _out_call(y1.reshape(t, B_WIDTH), b_w_out[0].astype(BF16), x2, g_post[1].reshape(1, d),
                   mod1, 2, seq, "attn_out_proj")
    return x2.reshape(bn, seq, d)
```

```python
import functools

import jax
import jax.numpy as jnp
from jax import lax
from jax.experimental import pallas as pl
from jax.experimental.pallas import tpu as pltpu

F32 = jnp.float32
BF16 = jnp.bfloat16

EPS = 1e-6
CHUNK = 64
A_HEADS = 8
A_DK = 256
A_DV = 512
A_QK = A_HEADS * A_DK
A_INNER = A_HEADS * A_DV
CONV_W = 4
B_HEADS = 16
B_DH = 128
B_WIDTH = B_HEADS * B_DH
LEFT_CHUNKS = 8
PAD = LEFT_CHUNKS * CHUNK
MAX_REL = 128
NEG = -1e30

LANES = 128
SUBLANES = 8

LOG2E = 1.4426950408889634

MLSTM_CHUNK = 256
MLSTM_HPG = 2
ATTN_LQ = 256
ATTN_HPG = 8
ATTN_WK = PAD + ATTN_LQ
ATTN_NB = (PAD + ATTN_WK) // LANES

VMEM_LIMIT = 56 * 1024 * 1024


def _sigmoid(v):
    return 0.5 * jnp.tanh(0.5 * v) + 0.5


def _silu(v):
    hv = 0.5 * v
    return hv * jnp.tanh(hv) + hv


def _log_sigmoid(v):
    return -(jnp.maximum(-v, 0.0) + jnp.log1p(jnp.exp(-jnp.abs(v))))


def _mod_kernel(c_ref, w_ref, b_ref, o_ref):
    sc = _silu(c_ref[...])
    o_ref[...] = jnp.dot(sc.astype(BF16), w_ref[...].astype(BF16),
                         preferred_element_type=F32) + b_ref[...]


def _mod_call(c8, w, b, layer=None):
    d = c8.shape[1]
    n = w.shape[-1]
    tn = 1024
    if layer is None:
        w_spec = pl.BlockSpec((d, tn), lambda j: (0, j))
        b_spec = pl.BlockSpec((1, tn), lambda j: (0, j))
        b2 = b.reshape(1, n)
    else:
        w_spec = pl.BlockSpec((None, d, tn), lambda j: (layer, 0, j))
        b_spec = pl.BlockSpec((None, 1, tn), lambda j: (layer, 0, j))
        b2 = b.reshape(b.shape[0], 1, n)
    return pl.pallas_call(
        _mod_kernel,
        grid=(n // tn,),
        in_specs=[pl.BlockSpec((SUBLANES, d), lambda j: (0, 0)), w_spec, b_spec],
        out_specs=pl.BlockSpec((SUBLANES, tn), lambda j: (0, j)),
        out_shape=jax.ShapeDtypeStruct((SUBLANES, n), F32),
        compiler_params=pltpu.CompilerParams(
            dimension_semantics=("arbitrary",), vmem_limit_bytes=VMEM_LIMIT),
        name="adaln_mod",
    )(c8, w, b2)


def _cast_kernel(w_ref, wg_ref, o_ref, og_ref):
    o_ref[...] = w_ref[...].astype(BF16)

    @pl.when(pl.program_id(0) == 0)
    def _():
        r = lax.broadcasted_iota(jnp.int32, wg_ref.shape, 0)
        og_ref[...] = jnp.where(r < 2 * A_HEADS, wg_ref[...], 0.0).astype(BF16)


def _cast_w_in_call(w_t, n_main):
    _, d = w_t.shape
    tn = 1024
    return pl.pallas_call(
        _cast_kernel,
        grid=(n_main // tn,),
        in_specs=[
            pl.BlockSpec((tn, d), lambda j: (j, 0)),
            pl.BlockSpec((LANES, d), lambda j: (n_main // LANES, 0)),
        ],
        out_specs=[
            pl.BlockSpec((tn, d), lambda j: (j, 0)),
            pl.BlockSpec((LANES, d), lambda j: (0, 0)),
        ],
        out_shape=[jax.ShapeDtypeStruct((n_main, d), BF16), jax.ShapeDtypeStruct((LANES, d), BF16)],
        compiler_params=pltpu.CompilerParams(
            dimension_semantics=("arbitrary",), vmem_limit_bytes=VMEM_LIMIT),
        name="cast_w_in",
    )(w_t, w_t)


_NT = (((1,), (1,)), ((), ()))


def _modulated(xn, g_ref, shift_ref, scale_ref):
    return ((xn * g_ref[...]) * (1.0 + scale_ref[...]) + shift_ref[...]).astype(BF16)


def _nmm_kernel(x_ref, g_ref, shift_ref, scale_ref, wt_ref, wgt_ref, o_ref, og_ref, h_scr):
    @pl.when(pl.program_id(1) == 0)
    def _():
        x = x_ref[...]
        xn = x * lax.rsqrt(jnp.mean(x * x, axis=-1, keepdims=True) + EPS)
        hb = _modulated(xn, g_ref, shift_ref, scale_ref)
        h_scr[...] = hb
        og_ref[...] = lax.dot_general(hb, wgt_ref[...], _NT, preferred_element_type=F32)

    o_ref[...] = lax.dot_general(h_scr[...], wt_ref[...], _NT, preferred_element_type=F32).astype(o_ref.dtype)


def _nmm_call(x2, g, mod3, w_t, wg_t, seq):
    t, d = x2.shape
    n = w_t.shape[0]
    tm, tn = 1024, 2048
    per_batch = seq // tm
    return pl.pallas_call(
        _nmm_kernel,
        grid=(t // tm, n // tn),
        in_specs=[
            pl.BlockSpec((tm, d), lambda i, j: (i, 0)),
            pl.BlockSpec((1, d), lambda i, j: (0, 0)),
            pl.BlockSpec((None, 1, d), lambda i, j: (i // per_batch, 0, 0)),
            pl.BlockSpec((None, 1, d), lambda i, j: (i // per_batch, 0, 1)),
            pl.BlockSpec((tn, d), lambda i, j: (j, 0)),
            pl.BlockSpec((LANES, d), lambda i, j: (0, 0)),
        ],
        out_specs=[pl.BlockSpec((tm, tn), lambda i, j: (i, j)),
                   pl.BlockSpec((tm, LANES), lambda i, j: (i, 0))],
        out_shape=[jax.ShapeDtypeStruct((t, n), BF16), jax.ShapeDtypeStruct((t, LANES), F32)],
        scratch_shapes=[pltpu.VMEM((tm, d), BF16)],
        compiler_params=pltpu.CompilerParams(
            dimension_semantics=("arbitrary", "arbitrary"), vmem_limit_bytes=VMEM_LIMIT),
        name="mlstm_in_proj",
    )(x2, g, mod3, mod3, w_t, wg_t)


def _nmm2_kernel(x_ref, g1_ref, shift1_ref, scale1_ref, g2_ref, shift2_ref, scale2_ref, w_ref,
                 o_ref, h1_scr, h2_scr, *, n1_tiles, lead_tiles, lead_scale):
    j = pl.program_id(1)

    @pl.when(j == 0)
    def _():
        x = x_ref[...]
        xn = x * lax.rsqrt(jnp.mean(x * x, axis=-1, keepdims=True) + EPS)
        h1_scr[...] = _modulated(xn, g1_ref, shift1_ref, scale1_ref)
        h2_scr[...] = _modulated(xn, g2_ref, shift2_ref, scale2_ref)

    @pl.when(j < n1_tiles)
    def _():
        acc = jnp.dot(h1_scr[...], w_ref[...], preferred_element_type=F32)
        acc = acc * jnp.where(j < lead_tiles, lead_scale, 1.0)
        o_ref[...] = acc.astype(o_ref.dtype)

    @pl.when(j >= n1_tiles)
    def _():
        o_ref[...] = jnp.dot(h2_scr[...], w_ref[...], preferred_element_type=F32).astype(o_ref.dtype)


def _nmm2_call(x2, g1, mod1, g2, mod2, w, n1, seq, lead_cols, lead_scale):
    t, d = x2.shape
    n = w.shape[1]
    tm, tn = 1024, 1024
    assert lead_cols % tn == 0 and n1 % tn == 0 and n % tn == 0
    per_batch = seq // tm
    n1_tiles = n1 // tn
    vec = lambda blk: pl.BlockSpec((None, 1, d), lambda i, j: (i // per_batch, 0, blk))
    return pl.pallas_call(
        functools.partial(_nmm2_kernel, n1_tiles=n1_tiles, lead_tiles=lead_cols // tn, lead_scale=lead_scale),
        grid=(t // tm, n // tn),
        in_specs=[
            pl.BlockSpec((tm, d), lambda i, j: (i, 0)),
            pl.BlockSpec((1, d), lambda i, j: (0, 0)), vec(0), vec(1),
            pl.BlockSpec((1, d), lambda i, j: (0, 0)), vec(0), vec(1),
            pl.BlockSpec((d, tn), lambda i, j: (0, j)),
        ],
        out_specs=pl.BlockSpec((tm, tn), lambda i, j: (i, j)),
        out_shape=jax.ShapeDtypeStruct((t, n), BF16),
        scratch_shapes=[pltpu.VMEM((tm, d), BF16), pltpu.VMEM((tm, d), BF16)],
        compiler_params=pltpu.CompilerParams(
            dimension_semantics=("arbitrary", "arbitrary"), vmem_limit_bytes=VMEM_LIMIT),
        name="attn_qzkv_proj",
    )(x2, g1, mod1, mod1, g2, mod2, mod2, w)


def _mlstm_kernel(gb_ref, sh_ref, q_ref, k_ref, v_ref, o_ref, z_ref, gr_ref, cwq_ref, cwk_ref,
                  cbq_ref, cbk_ref, gh_ref, y_ref, c_scr, n_scr, m_scr, hq_scr, hk_scr):
    group = pl.program_id(1)
    L = q_ref.shape[0]
    hp = q_ref.shape[1] // A_DK

    @pl.when(pl.program_id(2) == 0)
    def _():
        c_scr[...] = jnp.zeros_like(c_scr)
        n_scr[...] = jnp.zeros_like(n_scr)
        m_scr[...] = jnp.zeros_like(m_scr)
        hq_scr[...] = jnp.zeros_like(hq_scr)
        hk_scr[...] = jnp.zeros_like(hk_scr)

    row8 = lax.broadcasted_iota(jnp.int32, (SUBLANES, q_ref.shape[1]), 0)

    def conv_silu(u_ref, halo_scr, w_ref, b_ref):
        ub = u_ref[...]
        u = ub.astype(F32)
        shifted = jnp.dot(sh_ref[...], ub, preferred_element_type=F32)
        halo = halo_scr[...]
        acc = b_ref[...]
        for j in range(CONV_W - 1):
            s = CONV_W - 1 - j
            blk = shifted[(s - 1) * L:s * L]
            fix = jnp.where(row8 < s, pltpu.roll(halo, s, 0), 0.0)
            blk = jnp.concatenate([blk[:SUBLANES] + fix, blk[SUBLANES:]], axis=0)
            acc = acc + blk * w_ref[j:j + 1, :]
        acc = acc + u * w_ref[CONV_W - 1:CONV_W, :]
        halo_scr[...] = u[L - SUBLANES:]
        return _silu(acc)

    q_all = conv_silu(q_ref, hq_scr, cwq_ref, cbq_ref)
    k_all = conv_silu(k_ref, hk_scr, cwk_ref, cbk_ref) * (A_DK ** -0.5)

    row = lax.broadcasted_iota(jnp.int32, (L, L), 0)
    col = lax.broadcasted_iota(jnp.int32, (L, L), 1)
    causal = col <= row
    eye = col == row

    for h in range(hp):
        head = group * hp + h
        q = q_all[:, h * A_DK:(h + 1) * A_DK]
        k = k_all[:, h * A_DK:(h + 1) * A_DK]
        vcols = slice(h * A_DV, (h + 1) * A_DV)
        vb = v_ref[:, vcols]

        li_row = gr_ref[0, h] + gb_ref[head]
        lf_row = _log_sigmoid(gr_ref[1, h] + gb_ref[A_HEADS + head])

        b_col = jnp.sum(jnp.where(causal, lf_row, 0.0), axis=1, keepdims=True)
        b_row = jnp.sum(jnp.where(eye, b_col, 0.0), axis=0, keepdims=True)
        a_row = li_row - b_row
        dm = jnp.where(causal, b_col + a_row, NEG)
        m_prev = m_scr[h]
        inter = b_col + m_prev
        m_t = jnp.maximum(inter, jnp.max(dm, axis=1, keepdims=True))
        w_intra = jnp.exp(dm - m_t)
        w_inter = jnp.exp(inter - m_t)

        qb = q.astype(BF16)
        kb = k.astype(BF16)
        s = lax.dot_general(qb, kb, (((1,), (1,)), ((), ())), preferred_element_type=F32) * w_intra
        c_old = c_scr[h]
        n_old = n_scr[h]
        num = (w_inter * jnp.dot(qb, c_old.astype(BF16), preferred_element_type=F32)
               + jnp.dot(s.astype(BF16), vb, preferred_element_type=F32))
        den = (w_inter * jnp.sum(q * n_old, axis=1, keepdims=True)
               + jnp.sum(s, axis=1, keepdims=True))
        r = 1.0 / jnp.maximum(jnp.abs(den), jnp.exp(-m_t))
        rn = r * lax.rsqrt(r * r * jnp.mean(num * num, axis=-1, keepdims=True) + EPS)
        hn = num * rn * gh_ref[:, vcols]
        gate = _sigmoid(o_ref[:, vcols]) * _silu(z_ref[:, vcols])
        y = hn * gate.astype(F32)
        y_ref[:, vcols] = y.astype(y_ref.dtype)

        b_last = b_col[L - 1:L, :]
        g_row = b_last + a_row
        m_new = jnp.maximum(b_last + m_prev, jnp.max(g_row, axis=1, keepdims=True))
        decay = jnp.exp(b_last + m_prev - m_new)
        ws_row = jnp.exp(g_row - m_new)
        ws_col = jnp.sum(jnp.where(eye, ws_row, 0.0), axis=1, keepdims=True)
        kw = k * ws_col
        c_scr[h] = decay * c_old + lax.dot_general(
            kw.astype(BF16), vb, (((0,), (0,)), ((), ())), preferred_element_type=F32)
        n_scr[h] = decay * n_old + jnp.sum(kw, axis=0, keepdims=True)
        m_scr[h] = m_new


def _shift_matrix(L):
    t = jnp.arange(L)
    blocks = [(t[:, None] - s == t[None, :]) for s in range(1, CONV_W)]
    return jnp.concatenate(blocks, axis=0).astype(BF16)


def _mlstm_call(proj, gates_row, gate_b, conv_w, conv_b, g_head):
    bn, seq, _ = proj.shape
    L = MLSTM_CHUNK
    hp = MLSTM_HPG
    ng = A_HEADS // hp
    qw, vw = hp * A_DK, hp * A_DV
    v0 = 2 * A_QK // vw
    return pl.pallas_call(
        _mlstm_kernel,
        grid=(bn, ng, seq // L),
        in_specs=[
            pl.BlockSpec(memory_space=pltpu.SMEM),
            pl.BlockSpec(((CONV_W - 1) * L, L), lambda b, g, i: (0, 0)),
            pl.BlockSpec((None, L, qw), lambda b, g, i: (b, i, g)),
            pl.BlockSpec((None, L, qw), lambda b, g, i: (b, i, ng + g)),
            pl.BlockSpec((None, L, vw), lambda b, g, i: (b, i, v0 + g)),
            pl.BlockSpec((None, L, vw), lambda b, g, i: (b, i, v0 + ng + g)),
            pl.BlockSpec((None, L, vw), lambda b, g, i: (b, i, v0 + 2 * ng + g)),
            pl.BlockSpec((None, 2, hp, 1, L), lambda b, g, i: (b, 0, g, 0, i)),
            pl.BlockSpec((CONV_W, qw), lambda b, g, i: (0, g)),
            pl.BlockSpec((CONV_W, qw), lambda b, g, i: (0, ng + g)),
            pl.BlockSpec((1, qw), lambda b, g, i: (0, g)),
            pl.BlockSpec((1, qw), lambda b, g, i: (0, ng + g)),
            pl.BlockSpec((1, vw), lambda b, g, i: (0, g)),
        ],
        out_specs=pl.BlockSpec((None, L, vw), lambda b, g, i: (b, i, g)),
        out_shape=jax.ShapeDtypeStruct((bn, seq, A_INNER), BF16),
        scratch_shapes=[
            pltpu.VMEM((hp, A_DK, A_DV), F32),
            pltpu.VMEM((hp, 1, A_DK), F32),
            pltpu.VMEM((hp, 1, 1), F32),
            pltpu.VMEM((SUBLANES, qw), F32),
            pltpu.VMEM((SUBLANES, qw), F32),
        ],
        compiler_params=pltpu.CompilerParams(
            dimension_semantics=("arbitrary", "arbitrary", "arbitrary"), vmem_limit_bytes=VMEM_LIMIT),
        name="mlstm_core",
    )(gate_b, _shift_matrix(L), proj, proj, proj, proj, proj, gates_row, conv_w, conv_w, conv_b, conv_b, g_head)


def _out_kernel(y_ref, w_ref, x_ref, g_ref, gate_ref, o_ref):
    y = jnp.dot(y_ref[...], w_ref[...], preferred_element_type=F32)
    yn = y * lax.rsqrt(jnp.mean(y * y, axis=-1, keepdims=True) + EPS) * g_ref[...]
    o_ref[...] = x_ref[...] + gate_ref[...] * yn


def _out_call(y2, w, x2, g, mod3, gate_blk, seq, name):
    t, kdim = y2.shape
    d = w.shape[1]
    tm = 512
    per_batch = seq // tm
    return pl.pallas_call(
        _out_kernel,
        grid=(t // tm,),
        in_specs=[
            pl.BlockSpec((tm, kdim), lambda i: (i, 0)),
            pl.BlockSpec((kdim, d), lambda i: (0, 0), pipeline_mode=pl.Buffered(1)),
            pl.BlockSpec((tm, d), lambda i: (i, 0)),
            pl.BlockSpec((1, d), lambda i: (0, 0)),
            pl.BlockSpec((None, 1, d), lambda i: (i // per_batch, 0, gate_blk)),
        ],
        out_specs=pl.BlockSpec((tm, d), lambda i: (i, 0)),
        out_shape=jax.ShapeDtypeStruct((t, d), F32),
        compiler_params=pltpu.CompilerParams(
            dimension_semantics=("arbitrary",), vmem_limit_bytes=VMEM_LIMIT),
        name=name,
    )(y2, w, x2, g, mod3)


def _bias_kernel(tbl_ref, o_ref, base_scr):
    head = pl.program_id(0)
    nb, lq, _ = o_ref.shape
    width = nb * LANES + lq

    @pl.when(head == 0)
    def _():
        u = lax.broadcasted_iota(jnp.int32, (tbl_ref.shape[1], width), 1)
        u = jnp.where(u >= nb * LANES, u - width, u)
        idx = jnp.clip(PAD - u, -MAX_REL, MAX_REL) + MAX_REL
        c = lax.broadcasted_iota(jnp.int32, (tbl_ref.shape[1], width), 0)
        onehot = jnp.where(c == idx, 1.0, 0.0)
        base_scr[...] = jnp.dot(tbl_ref[...], onehot, preferred_element_type=F32,
                                precision=lax.Precision.HIGHEST)

    base = base_scr[pl.ds(head, 1), :]
    tile = jnp.broadcast_to(base, (lq, width))
    r = lax.broadcasted_iota(jnp.int32, (lq, width), 0)
    tile = pltpu.roll(tile, 0, 1, stride=1, stride_axis=0)
    uu = lax.broadcasted_iota(jnp.int32, (lq, width), 1)
    cq = r // CHUNK
    ck = uu // CHUNK - LEFT_CHUNKS
    visible = (ck <= cq) & (ck >= cq - LEFT_CHUNKS)
    tile = jnp.where(visible, tile * LOG2E, NEG)
    for j in range(nb):
        o_ref[j] = tile[:, j * LANES:(j + 1) * LANES]


def _bias_call(rel_table):
    nrel = rel_table.shape[1]
    ncols = -(-nrel // LANES) * LANES
    tbl = jnp.pad(rel_table, ((0, 0), (0, ncols - nrel)))
    return pl.pallas_call(
        _bias_kernel,
        grid=(B_HEADS,),
        in_specs=[pl.BlockSpec((B_HEADS, ncols), lambda h: (0, 0))],
        out_specs=pl.BlockSpec((None, ATTN_NB, ATTN_LQ, LANES), lambda h: (h, 0, 0, 0)),
        out_shape=jax.ShapeDtypeStruct((B_HEADS, ATTN_NB, ATTN_LQ, LANES), F32),
        scratch_shapes=[pltpu.VMEM((B_HEADS, ATTN_NB * LANES + ATTN_LQ), F32)],
        compiler_params=pltpu.CompilerParams(
            dimension_semantics=("arbitrary",), vmem_limit_bytes=VMEM_LIMIT),
        name="relpos_bias",
    )(tbl)


def _attn_kernel(q_ref, z_ref, k_ref, v_ref, wb_ref, y_ref):
    lq = q_ref.shape[0]
    wk = PAD + lq
    start = pl.program_id(2) * lq
    wstart = pl.multiple_of(jnp.maximum(start - PAD, 0), LANES)
    off = jnp.maximum(PAD - start, 0) // LANES
    for h in range(q_ref.shape[1] // B_DH):
        cols = slice(h * B_DH, (h + 1) * B_DH)
        kwin = k_ref[pl.ds(wstart, wk), cols]
        vwin = v_ref[pl.ds(wstart, wk), cols]
        s = lax.dot_general(q_ref[:, cols], kwin, (((1,), (1,)), ((), ())), preferred_element_type=F32)
        s = s + jnp.concatenate([wb_ref[h, off + j] for j in range(wk // LANES)], axis=1)
        m = jnp.max(s, axis=1, keepdims=True)
        p = jnp.exp2(s - m)
        inv = 1.0 / jnp.sum(p, axis=1, keepdims=True)
        o = jnp.dot(p.astype(BF16), vwin, preferred_element_type=F32) * inv
        y_ref[:, cols] = (o * _silu(z_ref[:, cols].astype(F32))).astype(y_ref.dtype)


def _attn_call(qzkv, wb):
    bn, seq, _ = qzkv.shape
    lq = ATTN_LQ
    gw = ATTN_HPG * B_DH
    ng = B_HEADS // ATTN_HPG
    resident = pl.Buffered(1)
    return pl.pallas_call(
        _attn_kernel,
        grid=(ng, bn, seq // lq),
        in_specs=[
            pl.BlockSpec((None, lq, gw), lambda g, b, i: (b, i, g)),
            pl.BlockSpec((None, lq, gw), lambda g, b, i: (b, i, ng + g)),
            pl.BlockSpec((None, seq, gw), lambda g, b, i: (b, 0, 2 * ng + g), pipeline_mode=resident),
            pl.BlockSpec((None, seq, gw), lambda g, b, i: (b, 0, 3 * ng + g), pipeline_mode=resident),
            pl.BlockSpec((ATTN_HPG, ATTN_NB, lq, LANES), lambda g, b, i: (g, 0, 0, 0), pipeline_mode=resident),
        ],
        out_specs=pl.BlockSpec((None, lq, gw), lambda g, b, i: (b, i, g)),
        out_shape=jax.ShapeDtypeStruct((bn, seq, B_WIDTH), BF16),
        compiler_params=pltpu.CompilerParams(
            dimension_semantics=("arbitrary", "arbitrary", "arbitrary"), vmem_limit_bytes=VMEM_LIMIT),
        name="band_attention",
    )(qzkv, qzkv, qzkv, qzkv, wb)


def kernel(x, c, ada_w, ada_b, g_pre, g_post, a_w_in, a_conv_w, a_conv_b, a_gate_b, a_g_head, a_w_out,
           kv_ada_w, kv_ada_b, kv_g, kv_w, b_w_in, b_rel, b_w_out):
    bn, seq, d = x.shape
    t = bn * seq
    assert bn <= SUBLANES and seq % 1024 == 0 and seq >= PAD + ATTN_LQ
    x2 = x.reshape(t, d)
    c8 = jnp.pad(c.astype(F32), ((0, SUBLANES - bn), (0, 0)))

    mod0 = _mod_call(c8, ada_w, ada_b, layer=0).reshape(SUBLANES, 1, 3 * d)
    n_main = 2 * A_QK + 3 * A_INNER
    w_main_t, w_gate_t = _cast_w_in_call(a_w_in[0].T, n_main)
    proj, gates = _nmm_call(x2, g_pre[0].reshape(1, d), mod0, w_main_t, w_gate_t, seq)
    gates_row = gates[:, :2 * A_HEADS].reshape(bn, seq, 2, A_HEADS).transpose(0, 2, 3, 1)
    gates_row = gates_row.reshape(bn, 2, A_HEADS, 1, seq)
    y0 = _mlstm_call(proj.reshape(bn, seq, n_main), gates_row, a_gate_b[0], a_conv_w[0],
                     a_conv_b[0].reshape(1, 2 * A_QK), a_g_head[0].reshape(1, A_INNER))
    x2 = _out_call(y0.reshape(t, A_INNER), a_w_out[0].astype(BF16), x2, g_post[0].reshape(1, d),
                   mod0, 2, seq, "mlstm_out_proj")

    mod1 = _mod_call(c8, ada_w, ada_b, layer=1).reshape(SUBLANES, 1, 3 * d)
    kmod = _mod_call(c8, kv_ada_w, kv_ada_b).reshape(SUBLANES, 1, 2 * d)
    w_qzkv = jnp.concatenate([b_w_in[0].astype(BF16), kv_w.astype(BF16)], axis=1)
    qzkv = _nmm2_call(x2, g_pre[1].reshape(1, d), mod1, kv_g.reshape(1, d), kmod, w_qzkv, 2 * B_WIDTH, seq,
                      lead_cols=B_WIDTH, lead_scale=(B_DH ** -0.5) * LOG2E)
    wb = _bias_call(b_rel[0])
    y1 = _attn_call(qzkv.reshape(bn, seq, 4 * B_WIDTH), wb)
    x2 = _out_call(y1.reshape(t, B_WIDTH), b_w_out[0].astype(BF16), x2, g_post[1].reshape(1, d),
                   mod1, 2, seq, "attn_out_proj")
    return x2.reshape(bn, seq, d)
```

```python
import functools

import jax
import jax.numpy as jnp
from jax import lax
from jax.experimental import pallas as pl
from jax.experimental.pallas import tpu as pltpu

F32 = jnp.float32
BF16 = jnp.bfloat16

EPS = 1e-6
CHUNK = 64
A_HEADS = 8
A_DK = 256
A_DV = 512
A_QK = A_HEADS * A_DK
A_INNER = A_HEADS * A_DV
CONV_W = 4
B_HEADS = 16
B_DH = 128
B_WIDTH = B_HEADS * B_DH
LEFT_CHUNKS = 8
PAD = LEFT_CHUNKS * CHUNK
MAX_REL = 128
NEG = -1e30

LANES = 128
SUBLANES = 8

LOG2E = 1.4426950408889634

MLSTM_CHUNK = 256
MLSTM_HPG = 2
ATTN_LQ = 256
ATTN_HPG = 8
ATTN_WK = PAD + ATTN_LQ
ATTN_NB = (PAD + ATTN_WK) // LANES

VMEM_LIMIT = 56 * 1024 * 1024


def _sigmoid(v):
    return 0.5 * jnp.tanh(0.5 * v) + 0.5


def _silu(v):
    hv = 0.5 * v
    return hv * jnp.tanh(hv) + hv


def _log_sigmoid(v):
    return -(jnp.maximum(-v, 0.0) + jnp.log1p(jnp.exp(-jnp.abs(v))))


def _mod_kernel(c_ref, w_ref, b_ref, o_ref):
    sc = _silu(c_ref[...])
    o_ref[...] = jnp.dot(sc.astype(BF16), w_ref[...].astype(BF16),
                         preferred_element_type=F32) + b_ref[...]


def _mod_call(c8, w, b, layer=None):
    d = c8.shape[1]
    n = w.shape[-1]
    tn = 1024
    if layer is None:
        w_spec = pl.BlockSpec((d, tn), lambda j: (0, j))
        b_spec = pl.BlockSpec((1, tn), lambda j: (0, j))
        b2 = b.reshape(1, n)
    else:
        w_spec = pl.BlockSpec((None, d, tn), lambda j: (layer, 0, j))
        b_spec = pl.BlockSpec((None, 1, tn), lambda j: (layer, 0, j))
        b2 = b.reshape(b.shape[0], 1, n)
    return pl.pallas_call(
        _mod_kernel,
        grid=(n // tn,),
        in_specs=[pl.BlockSpec((SUBLANES, d), lambda j: (0, 0)), w_spec, b_spec],
        out_specs=pl.BlockSpec((SUBLANES, tn), lambda j: (0, j)),
        out_shape=jax.ShapeDtypeStruct((SUBLANES, n), F32),
        compiler_params=pltpu.CompilerParams(
            dimension_semantics=("arbitrary",), vmem_limit_bytes=VMEM_LIMIT),
        name="adaln_mod",
    )(c8, w, b2)


def _cast_kernel(w_ref, wg_ref, o_ref, og_ref):
    o_ref[...] = w_ref[...].astype(BF16)

    @pl.when(pl.program_id(0) == 0)
    def _():
        r = lax.broadcasted_iota(jnp.int32, wg_ref.shape, 0)
        og_ref[...] = jnp.where(r < 2 * A_HEADS, wg_ref[...], 0.0).astype(BF16)


def _cast_w_in_call(w_t, n_main):
    _, d = w_t.shape
    tn = 1024
    return pl.pallas_call(
        _cast_kernel,
        grid=(n_main // tn,),
        in_specs=[
            pl.BlockSpec((tn, d), lambda j: (j, 0)),
            pl.BlockSpec((LANES, d), lambda j: (n_main // LANES, 0)),
        ],
        out_specs=[
            pl.BlockSpec((tn, d), lambda j: (j, 0)),
            pl.BlockSpec((LANES, d), lambda j: (0, 0)),
        ],
        out_shape=[jax.ShapeDtypeStruct((n_main, d), BF16), jax.ShapeDtypeStruct((LANES, d), BF16)],
        compiler_params=pltpu.CompilerParams(
            dimension_semantics=("arbitrary",), vmem_limit_bytes=VMEM_LIMIT),
        name="cast_w_in",
    )(w_t, w_t)


_NT = (((1,), (1,)), ((), ()))


def _modulated(xn, g_ref, shift_ref, scale_ref):
    return ((xn * g_ref[...]) * (1.0 + scale_ref[...]) + shift_ref[...]).astype(BF16)


def _proj_kernel(*refs, n_mod, trans_w, with_gates, n1_tiles, lead_tiles, lead_scale, rows):
    x_ref = refs[0]
    mod_refs = refs[1:1 + 3 * n_mod]
    rest = list(refs[1 + 3 * n_mod:])
    w_ref = rest.pop(0)
    wg_ref = rest.pop(0) if with_gates else None
    o_ref = rest.pop(0)
    og_ref = rest.pop(0) if with_gates else None
    h_scr = rest.pop(0)
    phase = pl.program_id(0)
    j = pl.program_id(1)
    slot = phase % 2

    def normalise_rows():
        r0 = pl.multiple_of(j * rows, rows)
        x = x_ref[pl.ds(r0, rows), :]
        xn = x * lax.rsqrt(jnp.mean(x * x, axis=-1, keepdims=True) + EPS)
        for k in range(n_mod):
            hb = _modulated(xn, *mod_refs[3 * k:3 * k + 3])
            h_scr[slot, k, pl.ds(r0, rows), :] = hb
            if with_gates and k == 0:
                og_ref[pl.ds(r0, rows), :] = lax.dot_general(hb, wg_ref[...], _NT, preferred_element_type=F32)

    def project():
        k = jnp.where(j >= n1_tiles, 1, 0) if n_mod == 2 else 0
        h = h_scr[1 - slot, k]
        if trans_w:
            acc = lax.dot_general(h, w_ref[...], _NT, preferred_element_type=F32)
        else:
            acc = jnp.dot(h, w_ref[...], preferred_element_type=F32)
        if lead_tiles:
            acc = acc * jnp.where(j < lead_tiles, lead_scale, 1.0)
        o_ref[...] = acc.astype(o_ref.dtype)

    @pl.when(phase == 0)
    def _():
        normalise_rows()

    @pl.when(phase > 0)
    def _():
        project()
        normalise_rows()


def _proj_call(x2, mods, w, wg, seq, name, *, tn, trans_w=False, n1=None, lead_cols=0, lead_scale=1.0):
    t, d = x2.shape
    n = w.shape[0] if trans_w else w.shape[1]
    tm = 1024
    nt, nj = t // tm, n // tn
    per_batch = seq // tm
    n_mod = len(mods)
    n1 = n if n1 is None else n1
    assert lead_cols % tn == 0 and n1 % tn == 0 and n % tn == 0 and tm % nj == 0

    def row_tile(p):
        return jnp.minimum(p, nt - 1)

    vec = lambda blk: pl.BlockSpec((None, 1, d), lambda p, j: (row_tile(p) // per_batch, 0, blk))
    in_specs = [pl.BlockSpec((tm, d), lambda p, j: (row_tile(p), 0))]
    args = [x2]
    for g, mod3 in mods:
        in_specs += [pl.BlockSpec((1, d), lambda p, j: (0, 0)), vec(0), vec(1)]
        args += [g, mod3, mod3]
    in_specs.append(pl.BlockSpec((tn, d), lambda p, j: (j, 0)) if trans_w
                    else pl.BlockSpec((d, tn), lambda p, j: (0, j)))
    args.append(w)
    out_specs = pl.BlockSpec((tm, tn), lambda p, j: (jnp.maximum(p - 1, 0), jnp.where(p == 0, 0, j)))
    out_shape = jax.ShapeDtypeStruct((t, n), BF16)
    if wg is not None:
        in_specs.append(pl.BlockSpec((LANES, d), lambda p, j: (0, 0)))
        args.append(wg)
        out_specs = [out_specs, pl.BlockSpec((tm, LANES), lambda p, j: (row_tile(p), 0))]
        out_shape = [out_shape, jax.ShapeDtypeStruct((t, LANES), F32)]
    return pl.pallas_call(
        functools.partial(_proj_kernel, n_mod=n_mod, trans_w=trans_w, with_gates=wg is not None,
                          n1_tiles=n1 // tn, lead_tiles=lead_cols // tn, lead_scale=lead_scale, rows=tm // nj),
        grid=(nt + 1, nj),
        in_specs=in_specs,
        out_specs=out_specs,
        out_shape=out_shape,
        scratch_shapes=[pltpu.VMEM((2, n_mod, tm, d), BF16)],
        compiler_params=pltpu.CompilerParams(
            dimension_semantics=("arbitrary", "arbitrary"), vmem_limit_bytes=VMEM_LIMIT),
        name=name,
    )(*args)


def _mlstm_kernel(gb_ref, sh_ref, q_ref, k_ref, v_ref, o_ref, z_ref, gr_ref, cwq_ref, cwk_ref,
                  cbq_ref, cbk_ref, gh_ref, y_ref, c_scr, n_scr, m_scr, hq_scr, hk_scr):
    group = pl.program_id(1)
    L = q_ref.shape[0]
    hp = q_ref.shape[1] // A_DK

    @pl.when(pl.program_id(2) == 0)
    def _():
        c_scr[...] = jnp.zeros_like(c_scr)
        n_scr[...] = jnp.zeros_like(n_scr)
        m_scr[...] = jnp.zeros_like(m_scr)
        hq_scr[...] = jnp.zeros_like(hq_scr)
        hk_scr[...] = jnp.zeros_like(hk_scr)

    row8 = lax.broadcasted_iota(jnp.int32, (SUBLANES, q_ref.shape[1]), 0)

    def conv_silu(u_ref, halo_scr, w_ref, b_ref):
        ub = u_ref[...]
        u = ub.astype(F32)
        shifted = jnp.dot(sh_ref[...], ub, preferred_element_type=F32)
        halo = halo_scr[...]
        acc = b_ref[...]
        for j in range(CONV_W - 1):
            s = CONV_W - 1 - j
            blk = shifted[(s - 1) * L:s * L]
            fix = jnp.where(row8 < s, pltpu.roll(halo, s, 0), 0.0)
            blk = jnp.concatenate([blk[:SUBLANES] + fix, blk[SUBLANES:]], axis=0)
            acc = acc + blk * w_ref[j:j + 1, :]
        acc = acc + u * w_ref[CONV_W - 1:CONV_W, :]
        halo_scr[...] = u[L - SUBLANES:]
        return _silu(acc)

    q_all = conv_silu(q_ref, hq_scr, cwq_ref, cbq_ref)
    k_all = conv_silu(k_ref, hk_scr, cwk_ref, cbk_ref) * (A_DK ** -0.5)

    row = lax.broadcasted_iota(jnp.int32, (L, L), 0)
    col = lax.broadcasted_iota(jnp.int32, (L, L), 1)
    causal = col <= row
    eye = col == row

    for h in range(hp):
        head = group * hp + h
        q = q_all[:, h * A_DK:(h + 1) * A_DK]
        k = k_all[:, h * A_DK:(h + 1) * A_DK]
        vcols = slice(h * A_DV, (h + 1) * A_DV)
        vb = v_ref[:, vcols]

        li_row = gr_ref[0, h] + gb_ref[head]
        lf_row = _log_sigmoid(gr_ref[1, h] + gb_ref[A_HEADS + head])

        b_col = jnp.sum(jnp.where(causal, lf_row, 0.0), axis=1, keepdims=True)
        b_row = jnp.sum(jnp.where(eye, b_col, 0.0), axis=0, keepdims=True)
        a_row = li_row - b_row
        dm = jnp.where(causal, b_col + a_row, NEG)
        m_prev = m_scr[h]
        inter = b_col + m_prev
        m_t = jnp.maximum(inter, jnp.max(dm, axis=1, keepdims=True))
        w_intra = jnp.exp(dm - m_t)
        w_inter = jnp.exp(inter - m_t)

        qb = q.astype(BF16)
        kb = k.astype(BF16)
        s = lax.dot_general(qb, kb, (((1,), (1,)), ((), ())), preferred_element_type=F32) * w_intra
        c_old = c_scr[h]
        n_old = n_scr[h]
        num = (w_inter * jnp.dot(qb, c_old.astype(BF16), preferred_element_type=F32)
               + jnp.dot(s.astype(BF16), vb, preferred_element_type=F32))
        den = (w_inter * jnp.sum(q * n_old, axis=1, keepdims=True)
               + jnp.sum(s, axis=1, keepdims=True))
        r = 1.0 / jnp.maximum(jnp.abs(den), jnp.exp(-m_t))
        rn = r * lax.rsqrt(r * r * jnp.mean(num * num, axis=-1, keepdims=True) + EPS)
        hn = num * rn * gh_ref[:, vcols]
        gate = _sigmoid(o_ref[:, vcols]) * _silu(z_ref[:, vcols])
        y = hn * gate.astype(F32)
        y_ref[:, vcols] = y.astype(y_ref.dtype)

        b_last = b_col[L - 1:L, :]
        g_row = b_last + a_row
        m_new = jnp.maximum(b_last + m_prev, jnp.max(g_row, axis=1, keepdims=True))
        decay = jnp.exp(b_last + m_prev - m_new)
        ws_row = jnp.exp(g_row - m_new)
        ws_col = jnp.sum(jnp.where(eye, ws_row, 0.0), axis=1, keepdims=True)
        kw = k * ws_col
        c_scr[h] = decay * c_old + lax.dot_general(
            kw.astype(BF16), vb, (((0,), (0,)), ((), ())), preferred_element_type=F32)
        n_scr[h] = decay * n_old + jnp.sum(kw, axis=0, keepdims=True)
        m_scr[h] = m_new


def _shift_matrix(L):
    t = jnp.arange(L)
    blocks = [(t[:, None] - s == t[None, :]) for s in range(1, CONV_W)]
    return jnp.concatenate(blocks, axis=0).astype(BF16)


def _mlstm_call(proj, gates_row, gate_b, conv_w, conv_b, g_head):
    bn, seq, _ = proj.shape
    L = MLSTM_CHUNK
    hp = MLSTM_HPG
    ng = A_HEADS // hp
    qw, vw = hp * A_DK, hp * A_DV
    v0 = 2 * A_QK // vw
    return pl.pallas_call(
        _mlstm_kernel,
        grid=(bn, ng, seq // L),
        in_specs=[
            pl.BlockSpec(memory_space=pltpu.SMEM),
            pl.BlockSpec(((CONV_W - 1) * L, L), lambda b, g, i: (0, 0)),
            pl.BlockSpec((None, L, qw), lambda b, g, i: (b, i, g)),
            pl.BlockSpec((None, L, qw), lambda b, g, i: (b, i, ng + g)),
            pl.BlockSpec((None, L, vw), lambda b, g, i: (b, i, v0 + g)),
            pl.BlockSpec((None, L, vw), lambda b, g, i: (b, i, v0 + ng + g)),
            pl.BlockSpec((None, L, vw), lambda b, g, i: (b, i, v0 + 2 * ng + g)),
            pl.BlockSpec((None, 2, hp, 1, L), lambda b, g, i: (b, 0, g, 0, i)),
            pl.BlockSpec((CONV_W, qw), lambda b, g, i: (0, g)),
            pl.BlockSpec((CONV_W, qw), lambda b, g, i: (0, ng + g)),
            pl.BlockSpec((1, qw), lambda b, g, i: (0, g)),
            pl.BlockSpec((1, qw), lambda b, g, i: (0, ng + g)),
            pl.BlockSpec((1, vw), lambda b, g, i: (0, g)),
        ],
        out_specs=pl.BlockSpec((None, L, vw), lambda b, g, i: (b, i, g)),
        out_shape=jax.ShapeDtypeStruct((bn, seq, A_INNER), BF16),
        scratch_shapes=[
            pltpu.VMEM((hp, A_DK, A_DV), F32),
            pltpu.VMEM((hp, 1, A_DK), F32),
            pltpu.VMEM((hp, 1, 1), F32),
            pltpu.VMEM((SUBLANES, qw), F32),
            pltpu.VMEM((SUBLANES, qw), F32),
        ],
        compiler_params=pltpu.CompilerParams(
            dimension_semantics=("arbitrary", "arbitrary", "arbitrary"), vmem_limit_bytes=VMEM_LIMIT),
        name="mlstm_core",
    )(gate_b, _shift_matrix(L), proj, proj, proj, proj, proj, gates_row, conv_w, conv_w, conv_b, conv_b, g_head)


def _out_kernel(y_ref, w_ref, x_ref, g_ref, gate_ref, o_ref):
    y = jnp.dot(y_ref[...], w_ref[...], preferred_element_type=F32)
    yn = y * lax.rsqrt(jnp.mean(y * y, axis=-1, keepdims=True) + EPS) * g_ref[...]
    o_ref[...] = x_ref[...] + gate_ref[...] * yn


def _out_call(y2, w, x2, g, mod3, gate_blk, seq, name):
    t, kdim = y2.shape
    d = w.shape[1]
    tm = 512
    per_batch = seq // tm
    return pl.pallas_call(
        _out_kernel,
        grid=(t // tm,),
        in_specs=[
            pl.BlockSpec((tm, kdim), lambda i: (i, 0)),
            pl.BlockSpec((kdim, d), lambda i: (0, 0), pipeline_mode=pl.Buffered(1)),
            pl.BlockSpec((tm, d), lambda i: (i, 0)),
            pl.BlockSpec((1, d), lambda i: (0, 0)),
            pl.BlockSpec((None, 1, d), lambda i: (i // per_batch, 0, gate_blk)),
        ],
        out_specs=pl.BlockSpec((tm, d), lambda i: (i, 0)),
        out_shape=jax.ShapeDtypeStruct((t, d), F32),
        compiler_params=pltpu.CompilerParams(
            dimension_semantics=("arbitrary",), vmem_limit_bytes=VMEM_LIMIT),
        name=name,
    )(y2, w, x2, g, mod3)


def _bias_kernel(tbl_ref, o_ref, base_scr):
    head = pl.program_id(0)
    nb, lq, _ = o_ref.shape
    width = nb * LANES + lq

    @pl.when(head == 0)
    def _():
        u = lax.broadcasted_iota(jnp.int32, (tbl_ref.shape[1], width), 1)
        u = jnp.where(u >= nb * LANES, u - width, u)
        idx = jnp.clip(PAD - u, -MAX_REL, MAX_REL) + MAX_REL
        c = lax.broadcasted_iota(jnp.int32, (tbl_ref.shape[1], width), 0)
        onehot = jnp.where(c == idx, 1.0, 0.0)
        base_scr[...] = jnp.dot(tbl_ref[...], onehot, preferred_element_type=F32,
                                precision=lax.Precision.HIGHEST)

    base = base_scr[pl.ds(head, 1), :]
    tile = jnp.broadcast_to(base, (lq, width))
    r = lax.broadcasted_iota(jnp.int32, (lq, width), 0)
    tile = pltpu.roll(tile, 0, 1, stride=1, stride_axis=0)
    uu = lax.broadcasted_iota(jnp.int32, (lq, width), 1)
    cq = r // CHUNK
    ck = uu // CHUNK - LEFT_CHUNKS
    visible = (ck <= cq) & (ck >= cq - LEFT_CHUNKS)
    tile = jnp.where(visible, tile * LOG2E, NEG)
    for j in range(nb):
        o_ref[j] = tile[:, j * LANES:(j + 1) * LANES]


def _bias_call(rel_table):
    nrel = rel_table.shape[1]
    ncols = -(-nrel // LANES) * LANES
    tbl = jnp.pad(rel_table, ((0, 0), (0, ncols - nrel)))
    return pl.pallas_call(
        _bias_kernel,
        grid=(B_HEADS,),
        in_specs=[pl.BlockSpec((B_HEADS, ncols), lambda h: (0, 0))],
        out_specs=pl.BlockSpec((None, ATTN_NB, ATTN_LQ, LANES), lambda h: (h, 0, 0, 0)),
        out_shape=jax.ShapeDtypeStruct((B_HEADS, ATTN_NB, ATTN_LQ, LANES), F32),
        scratch_shapes=[pltpu.VMEM((B_HEADS, ATTN_NB * LANES + ATTN_LQ), F32)],
        compiler_params=pltpu.CompilerParams(
            dimension_semantics=("arbitrary",), vmem_limit_bytes=VMEM_LIMIT),
        name="relpos_bias",
    )(tbl)


def _attn_kernel(q_ref, z_ref, k_ref, v_ref, wb_ref, y_ref):
    lq = q_ref.shape[0]
    wk = PAD + lq
    start = pl.program_id(2) * lq
    wstart = pl.multiple_of(jnp.maximum(start - PAD, 0), LANES)
    off = jnp.maximum(PAD - start, 0) // LANES
    for h in range(q_ref.shape[1] // B_DH):
        cols = slice(h * B_DH, (h + 1) * B_DH)
        kwin = k_ref[pl.ds(wstart, wk), cols]
        vwin = v_ref[pl.ds(wstart, wk), cols]
        s = lax.dot_general(q_ref[:, cols], kwin, (((1,), (1,)), ((), ())), preferred_element_type=F32)
        s = s + jnp.concatenate([wb_ref[h, off + j] for j in range(wk // LANES)], axis=1)
        m = jnp.max(s, axis=1, keepdims=True)
        p = jnp.exp2(s - m)
        inv = 1.0 / jnp.sum(p, axis=1, keepdims=True)
        o = jnp.dot(p.astype(BF16), vwin, preferred_element_type=F32) * inv
        y_ref[:, cols] = (o * _silu(z_ref[:, cols].astype(F32))).astype(y_ref.dtype)


def _attn_call(qzkv, wb):
    bn, seq, _ = qzkv.shape
    lq = ATTN_LQ
    gw = ATTN_HPG * B_DH
    ng = B_HEADS // ATTN_HPG
    resident = pl.Buffered(1)
    return pl.pallas_call(
        _attn_kernel,
        grid=(ng, bn, seq // lq),
        in_specs=[
            pl.BlockSpec((None, lq, gw), lambda g, b, i: (b, i, g)),
            pl.BlockSpec((None, lq, gw), lambda g, b, i: (b, i, ng + g)),
            pl.BlockSpec((None, seq, gw), lambda g, b, i: (b, 0, 2 * ng + g), pipeline_mode=resident),
            pl.BlockSpec((None, seq, gw), lambda g, b, i: (b, 0, 3 * ng + g), pipeline_mode=resident),
            pl.BlockSpec((ATTN_HPG, ATTN_NB, lq, LANES), lambda g, b, i: (g, 0, 0, 0), pipeline_mode=resident),
        ],
        out_specs=pl.BlockSpec((None, lq, gw), lambda g, b, i: (b, i, g)),
        out_shape=jax.ShapeDtypeStruct((bn, seq, B_WIDTH), BF16),
        compiler_params=pltpu.CompilerParams(
            dimension_semantics=("arbitrary", "arbitrary", "arbitrary"), vmem_limit_bytes=VMEM_LIMIT),
        name="band_attention",
    )(qzkv, qzkv, qzkv, qzkv, wb)


def kernel(x, c, ada_w, ada_b, g_pre, g_post, a_w_in, a_conv_w, a_conv_b, a_gate_b, a_g_head, a_w_out,
           kv_ada_w, kv_ada_b, kv_g, kv_w, b_w_in, b_rel, b_w_out):
    bn, seq, d = x.shape
    t = bn * seq
    assert bn <= SUBLANES and seq % 1024 == 0 and seq >= PAD + ATTN_LQ
    x2 = x.reshape(t, d)
    c8 = jnp.pad(c.astype(F32), ((0, SUBLANES - bn), (0, 0)))

    mod0 = _mod_call(c8, ada_w, ada_b, layer=0).reshape(SUBLANES, 1, 3 * d)
    n_main = 2 * A_QK + 3 * A_INNER
    w_main_t, w_gate_t = _cast_w_in_call(a_w_in[0].T, n_main)
    proj, gates = _proj_call(x2, [(g_pre[0].reshape(1, d), mod0)], w_main_t, w_gate_t, seq, "mlstm_in_proj",
                             tn=2048, trans_w=True)
    gates_row = gates[:, :2 * A_HEADS].reshape(bn, seq, 2, A_HEADS).transpose(0, 2, 3, 1)
    gates_row = gates_row.reshape(bn, 2, A_HEADS, 1, seq)
    y0 = _mlstm_call(proj.reshape(bn, seq, n_main), gates_row, a_gate_b[0], a_conv_w[0],
                     a_conv_b[0].reshape(1, 2 * A_QK), a_g_head[0].reshape(1, A_INNER))
    x2 = _out_call(y0.reshape(t, A_INNER), a_w_out[0].astype(BF16), x2, g_post[0].reshape(1, d),
                   mod0, 2, seq, "mlstm_out_proj")

    mod1 = _mod_call(c8, ada_w, ada_b, layer=1).reshape(SUBLANES, 1, 3 * d)
    kmod = _mod_call(c8, kv_ada_w, kv_ada_b).reshape(SUBLANES, 1, 2 * d)
    w_qzkv = jnp.concatenate([b_w_in[0].astype(BF16), kv_w.astype(BF16)], axis=1)
    qzkv = _proj_call(x2, [(g_pre[1].reshape(1, d), mod1), (kv_g.reshape(1, d), kmod)], w_qzkv, None, seq,
                      "attn_qzkv_proj", tn=1024, n1=2 * B_WIDTH,
                      lead_cols=B_WIDTH, lead_scale=(B_DH ** -0.5) * LOG2E)
    wb = _bias_call(b_rel[0])
    y1 = _attn_call(qzkv.reshape(bn, seq, 4 * B_WIDTH), wb)
    x2 = _out_call(y1.reshape(t, B_WIDTH), b_w_out[0].astype(BF16), x2, g_post[1].reshape(1, d),
                   mod1, 2, seq, "attn_out_proj")
    return x2.reshape(bn, seq, d)
```

```python
import functools

import jax
import jax.numpy as jnp
from jax import lax
from jax.experimental import pallas as pl
from jax.experimental.pallas import tpu as pltpu

F32 = jnp.float32
BF16 = jnp.bfloat16

EPS = 1e-6
CHUNK = 64
A_HEADS = 8
A_DK = 256
A_DV = 512
A_QK = A_HEADS * A_DK
A_INNER = A_HEADS * A_DV
CONV_W = 4
B_HEADS = 16
B_DH = 128
B_WIDTH = B_HEADS * B_DH
LEFT_CHUNKS = 8
PAD = LEFT_CHUNKS * CHUNK
MAX_REL = 128
NEG = -1e30

LANES = 128
SUBLANES = 8

LOG2E = 1.4426950408889634

MLSTM_CHUNK = 256
MLSTM_HPG = 4
ATTN_LQ = 256
ATTN_HPG = 8
ATTN_WK = PAD + ATTN_LQ
ATTN_NB = (PAD + ATTN_WK) // LANES

VMEM_LIMIT = 56 * 1024 * 1024


def _sigmoid(v):
    return 0.5 * jnp.tanh(0.5 * v) + 0.5


def _silu(v):
    hv = 0.5 * v
    return hv * jnp.tanh(hv) + hv


def _log_sigmoid(v):
    return -(jnp.maximum(-v, 0.0) + jnp.log1p(jnp.exp(-jnp.abs(v))))


def _mod_kernel(c_ref, w_ref, b_ref, o_ref):
    sc = _silu(c_ref[...])
    o_ref[...] = jnp.dot(sc.astype(BF16), w_ref[...].astype(BF16),
                         preferred_element_type=F32) + b_ref[...]


def _mod_call(c8, w, b, layer=None):
    d = c8.shape[1]
    n = w.shape[-1]
    tn = 1024
    if layer is None:
        w_spec = pl.BlockSpec((d, tn), lambda j: (0, j))
        b_spec = pl.BlockSpec((1, tn), lambda j: (0, j))
        b2 = b.reshape(1, n)
    else:
        w_spec = pl.BlockSpec((None, d, tn), lambda j: (layer, 0, j))
        b_spec = pl.BlockSpec((None, 1, tn), lambda j: (layer, 0, j))
        b2 = b.reshape(b.shape[0], 1, n)
    return pl.pallas_call(
        _mod_kernel,
        grid=(n // tn,),
        in_specs=[pl.BlockSpec((SUBLANES, d), lambda j: (0, 0)), w_spec, b_spec],
        out_specs=pl.BlockSpec((SUBLANES, tn), lambda j: (0, j)),
        out_shape=jax.ShapeDtypeStruct((SUBLANES, n), F32),
        compiler_params=pltpu.CompilerParams(
            dimension_semantics=("arbitrary",), vmem_limit_bytes=VMEM_LIMIT),
        name="adaln_mod",
    )(c8, w, b2)


def _cast_kernel(w_ref, wg_ref, o_ref, og_ref):
    o_ref[...] = w_ref[...].astype(BF16)

    @pl.when(pl.program_id(0) == 0)
    def _():
        r = lax.broadcasted_iota(jnp.int32, wg_ref.shape, 0)
        og_ref[...] = jnp.where(r < 2 * A_HEADS, wg_ref[...], 0.0).astype(BF16)


def _cast_w_in_call(w_t, n_main):
    _, d = w_t.shape
    tn = 1024
    return pl.pallas_call(
        _cast_kernel,
        grid=(n_main // tn,),
        in_specs=[
            pl.BlockSpec((tn, d), lambda j: (j, 0)),
            pl.BlockSpec((LANES, d), lambda j: (n_main // LANES, 0)),
        ],
        out_specs=[
            pl.BlockSpec((tn, d), lambda j: (j, 0)),
            pl.BlockSpec((LANES, d), lambda j: (0, 0)),
        ],
        out_shape=[jax.ShapeDtypeStruct((n_main, d), BF16), jax.ShapeDtypeStruct((LANES, d), BF16)],
        compiler_params=pltpu.CompilerParams(
            dimension_semantics=("arbitrary",), vmem_limit_bytes=VMEM_LIMIT),
        name="cast_w_in",
    )(w_t, w_t)


def _cast_pair_kernel(w1_ref, w2_ref, o_ref, *, n1_tiles):
    @pl.when(pl.program_id(0) < n1_tiles)
    def _():
        o_ref[...] = w1_ref[...].astype(BF16)

    @pl.when(pl.program_id(0) >= n1_tiles)
    def _():
        o_ref[...] = w2_ref[...].astype(BF16)


def _cast_pair_call(w1, w2):
    d, n1 = w1.shape
    n2 = w2.shape[1]
    tn = 512
    n1_tiles = n1 // tn
    return pl.pallas_call(
        functools.partial(_cast_pair_kernel, n1_tiles=n1_tiles),
        grid=((n1 + n2) // tn,),
        in_specs=[
            pl.BlockSpec((d, tn), lambda j: (0, jnp.minimum(j, n1_tiles - 1))),
            pl.BlockSpec((d, tn), lambda j: (0, jnp.maximum(j - n1_tiles, 0))),
        ],
        out_specs=pl.BlockSpec((d, tn), lambda j: (0, j)),
        out_shape=jax.ShapeDtypeStruct((d, n1 + n2), BF16),
        compiler_params=pltpu.CompilerParams(
            dimension_semantics=("arbitrary",), vmem_limit_bytes=VMEM_LIMIT),
        name="cast_w_qzkv",
    )(w1, w2)


_NT =(((1,), (1,)), ((), ()))


def _modulated(xn, g_ref, shift_ref, scale_ref):
    return ((xn * g_ref[...]) * (1.0 + scale_ref[...]) + shift_ref[...]).astype(BF16)


def _proj_kernel(*refs, n_mod, trans_w, with_gates, n1_tiles, lead_tiles, lead_scale, rows):
    x_ref = refs[0]
    mod_refs = refs[1:1 + 3 * n_mod]
    rest = list(refs[1 + 3 * n_mod:])
    w_ref = rest.pop(0)
    wg_ref = rest.pop(0) if with_gates else None
    o_ref = rest.pop(0)
    og_ref = rest.pop(0) if with_gates else None
    h_scr = rest.pop(0)
    phase = pl.program_id(0)
    j = pl.program_id(1)
    slot = phase % 2

    def normalise_rows():
        r0 = pl.multiple_of(j * rows, rows)
        x = x_ref[pl.ds(r0, rows), :]
        xn = x * lax.rsqrt(jnp.mean(x * x, axis=-1, keepdims=True) + EPS)
        for k in range(n_mod):
            hb = _modulated(xn, *mod_refs[3 * k:3 * k + 3])
            h_scr[slot, k, pl.ds(r0, rows), :] = hb
            if with_gates and k == 0:
                og_ref[pl.ds(r0, rows), :] = lax.dot_general(hb, wg_ref[...], _NT, preferred_element_type=F32)

    def project():
        k = jnp.where(j >= n1_tiles, 1, 0) if n_mod == 2 else 0
        h = h_scr[1 - slot, k]
        if trans_w:
            acc = lax.dot_general(h, w_ref[...], _NT, preferred_element_type=F32)
        else:
            acc = jnp.dot(h, w_ref[...], preferred_element_type=F32)
        if lead_tiles:
            acc = acc * jnp.where(j < lead_tiles, lead_scale, 1.0)
        o_ref[...] = acc.astype(o_ref.dtype)

    @pl.when(phase == 0)
    def _():
        normalise_rows()

    @pl.when(phase > 0)
    def _():
        project()
        normalise_rows()


def _proj_call(x2, mods, w, wg, seq, name, *, tn, trans_w=False, n1=None, lead_cols=0, lead_scale=1.0):
    t, d = x2.shape
    n = w.shape[0] if trans_w else w.shape[1]
    tm = 1024
    nt, nj = t // tm, n // tn
    per_batch = seq // tm
    n_mod = len(mods)
    n1 = n if n1 is None else n1
    assert lead_cols % tn == 0 and n1 % tn == 0 and n % tn == 0 and tm % nj == 0

    def row_tile(p):
        return jnp.minimum(p, nt - 1)

    vec = lambda blk: pl.BlockSpec((None, 1, d), lambda p, j: (row_tile(p) // per_batch, 0, blk))
    in_specs = [pl.BlockSpec((tm, d), lambda p, j: (row_tile(p), 0))]
    args = [x2]
    for g, mod3 in mods:
        in_specs += [pl.BlockSpec((1, d), lambda p, j: (0, 0)), vec(0), vec(1)]
        args += [g, mod3, mod3]
    in_specs.append(pl.BlockSpec((tn, d), lambda p, j: (j, 0)) if trans_w
                    else pl.BlockSpec((d, tn), lambda p, j: (0, j)))
    args.append(w)
    out_specs = pl.BlockSpec((tm, tn), lambda p, j: (jnp.maximum(p - 1, 0), jnp.where(p == 0, 0, j)))
    out_shape = jax.ShapeDtypeStruct((t, n), BF16)
    if wg is not None:
        in_specs.append(pl.BlockSpec((LANES, d), lambda p, j: (0, 0)))
        args.append(wg)
        out_specs = [out_specs, pl.BlockSpec((tm, LANES), lambda p, j: (row_tile(p), 0))]
        out_shape = [out_shape, jax.ShapeDtypeStruct((t, LANES), F32)]
    return pl.pallas_call(
        functools.partial(_proj_kernel, n_mod=n_mod, trans_w=trans_w, with_gates=wg is not None,
                          n1_tiles=n1 // tn, lead_tiles=lead_cols // tn, lead_scale=lead_scale, rows=tm // nj),
        grid=(nt + 1, nj),
        in_specs=in_specs,
        out_specs=out_specs,
        out_shape=out_shape,
        scratch_shapes=[pltpu.VMEM((2, n_mod, tm, d), BF16)],
        compiler_params=pltpu.CompilerParams(
            dimension_semantics=("arbitrary", "arbitrary"), vmem_limit_bytes=VMEM_LIMIT),
        name=name,
    )(*args)


def _mlstm_kernel(gb_ref, sh_ref, q_ref, k_ref, v_ref, o_ref, z_ref, gr_ref, cwq_ref, cwk_ref,
                  cbq_ref, cbk_ref, gh_ref, y_ref, c_scr, n_scr, m_scr, hq_scr, hk_scr):
    group = pl.program_id(1)
    L = q_ref.shape[0]
    hp = q_ref.shape[1] // A_DK

    @pl.when(pl.program_id(2) == 0)
    def _():
        c_scr[...] = jnp.zeros_like(c_scr)
        n_scr[...] = jnp.zeros_like(n_scr)
        m_scr[...] = jnp.zeros_like(m_scr)
        hq_scr[...] = jnp.zeros_like(hq_scr)
        hk_scr[...] = jnp.zeros_like(hk_scr)

    row8 = lax.broadcasted_iota(jnp.int32, (SUBLANES, LANES), 0)

    def conv_silu(u_ref, halo_scr, w_ref, b_ref):
        blocks = []
        for cb in range(u_ref.shape[1] // LANES):
            cols = slice(cb * LANES, (cb + 1) * LANES)
            ub = u_ref[:, cols]
            u = ub.astype(F32)
            shifted = jnp.dot(sh_ref[...], ub, preferred_element_type=F32)
            halo = halo_scr[:, cols]
            acc = b_ref[:, cols]
            for j in range(CONV_W - 1):
                s = CONV_W - 1 - j
                blk = shifted[(s - 1) * L:s * L]
                fix = jnp.where(row8 < s, pltpu.roll(halo, s, 0), 0.0)
                blk = jnp.concatenate([blk[:SUBLANES] + fix, blk[SUBLANES:]], axis=0)
                acc = acc + blk * w_ref[j:j + 1, cols]
            acc = acc + u * w_ref[CONV_W - 1:CONV_W, cols]
            halo_scr[:, cols] = u[L - SUBLANES:]
            blocks.append(_silu(acc))
        bph = A_DK // LANES
        return [jnp.concatenate(blocks[h * bph:(h + 1) * bph], axis=1) for h in range(len(blocks) // bph)]

    q_heads = conv_silu(q_ref, hq_scr, cwq_ref, cbq_ref)
    k_heads = [kh * (A_DK ** -0.5) for kh in conv_silu(k_ref, hk_scr, cwk_ref, cbk_ref)]

    row = lax.broadcasted_iota(jnp.int32, (L, L), 0)
    col = lax.broadcasted_iota(jnp.int32, (L, L), 1)
    causal = col <= row
    eye = col == row

    for h in range(hp):
        head = group * hp + h
        q = q_heads[h]
        k = k_heads[h]
        vcols = slice(h * A_DV, (h + 1) * A_DV)
        vb = v_ref[:, vcols]

        li_row = gr_ref[0, h] + gb_ref[head]
        lf_row = _log_sigmoid(gr_ref[1, h] + gb_ref[A_HEADS + head])

        b_col = jnp.sum(jnp.where(causal, lf_row, 0.0), axis=1, keepdims=True)
        b_row = jnp.sum(jnp.where(eye, b_col, 0.0), axis=0, keepdims=True)
        a_row = li_row - b_row
        dm = jnp.where(causal, b_col + a_row, NEG)
        m_prev = m_scr[h]
        inter = b_col + m_prev
        m_t = jnp.maximum(inter, jnp.max(dm, axis=1, keepdims=True))
        w_intra = jnp.exp(dm - m_t)
        w_inter = jnp.exp(inter - m_t)

        qb = q.astype(BF16)
        kb = k.astype(BF16)
        s = lax.dot_general(qb, kb, (((1,), (1,)), ((), ())), preferred_element_type=F32) * w_intra
        c_old = c_scr[h]
        n_old = n_scr[h]
        num = (w_inter * jnp.dot(qb, c_old.astype(BF16), preferred_element_type=F32)
               + jnp.dot(s.astype(BF16), vb, preferred_element_type=F32))
        den = (w_inter * jnp.sum(q * n_old, axis=1, keepdims=True)
               + jnp.sum(s, axis=1, keepdims=True))
        r = 1.0 / jnp.maximum(jnp.abs(den), jnp.exp(-m_t))
        rn = r * lax.rsqrt(r * r * jnp.mean(num * num, axis=-1, keepdims=True) + EPS)
        hn = num * rn * gh_ref[:, vcols]
        gate = _sigmoid(o_ref[:, vcols]) * _silu(z_ref[:, vcols])
        y = hn * gate.astype(F32)
        y_ref[:, vcols] = y.astype(y_ref.dtype)

        b_last = b_col[L - 1:L, :]
        g_row = b_last + a_row
        m_new = jnp.maximum(b_last + m_prev, jnp.max(g_row, axis=1, keepdims=True))
        decay = jnp.exp(b_last + m_prev - m_new)
        ws_row = jnp.exp(g_row - m_new)
        ws_col = jnp.sum(jnp.where(eye, ws_row, 0.0), axis=1, keepdims=True)
        kw = k * ws_col
        c_scr[h] = decay * c_old + lax.dot_general(
            kw.astype(BF16), vb, (((0,), (0,)), ((), ())), preferred_element_type=F32)
        n_scr[h] = decay * n_old + jnp.sum(kw, axis=0, keepdims=True)
        m_scr[h] = m_new


def _shift_matrix(L):
    t = jnp.arange(L)
    blocks = [(t[:, None] - s == t[None, :]) for s in range(1, CONV_W)]
    return jnp.concatenate(blocks, axis=0).astype(BF16)


def _mlstm_call(proj, gates_row, gate_b, conv_w, conv_b, g_head):
    bn, seq, _ = proj.shape
    L = MLSTM_CHUNK
    hp = MLSTM_HPG
    ng = A_HEADS // hp
    qw, vw = hp * A_DK, hp * A_DV
    v0 = 2 * A_QK // vw
    return pl.pallas_call(
        _mlstm_kernel,
        grid=(bn, ng, seq // L),
        in_specs=[
            pl.BlockSpec(memory_space=pltpu.SMEM),
            pl.BlockSpec(((CONV_W - 1) * L, L), lambda b, g, i: (0, 0)),
            pl.BlockSpec((None, L, qw), lambda b, g, i: (b, i, g)),
            pl.BlockSpec((None, L, qw), lambda b, g, i: (b, i, ng + g)),
            pl.BlockSpec((None, L, vw), lambda b, g, i: (b, i, v0 + g)),
            pl.BlockSpec((None, L, vw), lambda b, g, i: (b, i, v0 + ng + g)),
            pl.BlockSpec((None, L, vw), lambda b, g, i: (b, i, v0 + 2 * ng + g)),
            pl.BlockSpec((None, 2, hp, 1, L), lambda b, g, i: (b, 0, g, 0, i)),
            pl.BlockSpec((CONV_W, qw), lambda b, g, i: (0, g)),
            pl.BlockSpec((CONV_W, qw), lambda b, g, i: (0, ng + g)),
            pl.BlockSpec((1, qw), lambda b, g, i: (0, g)),
            pl.BlockSpec((1, qw), lambda b, g, i: (0, ng + g)),
            pl.BlockSpec((1, vw), lambda b, g, i: (0, g)),
        ],
        out_specs=pl.BlockSpec((None, L, vw), lambda b, g, i: (b, i, g)),
        out_shape=jax.ShapeDtypeStruct((bn, seq, A_INNER), BF16),
        scratch_shapes=[
            pltpu.VMEM((hp, A_DK, A_DV), F32),
            pltpu.VMEM((hp, 1, A_DK), F32),
            pltpu.VMEM((hp, 1, 1), F32),
            pltpu.VMEM((SUBLANES, qw), F32),
            pltpu.VMEM((SUBLANES, qw), F32),
        ],
        compiler_params=pltpu.CompilerParams(
            dimension_semantics=("arbitrary", "arbitrary", "arbitrary"), vmem_limit_bytes=VMEM_LIMIT),
        name="mlstm_core",
    )(gate_b, _shift_matrix(L), proj, proj, proj, proj, proj, gates_row, conv_w, conv_w, conv_b, conv_b, g_head)


def _out_kernel(y_ref, w_ref, x_ref, g_ref, gate_ref, o_ref):
    y = jnp.dot(y_ref[...], w_ref[...], preferred_element_type=F32)
    yn = y * lax.rsqrt(jnp.mean(y * y, axis=-1, keepdims=True) + EPS) * g_ref[...]
    o_ref[...] = x_ref[...] + gate_ref[...] * yn


def _out_call(y2, w, x2, g, mod3, gate_blk, seq, name):
    t, kdim = y2.shape
    d = w.shape[1]
    tm = 512
    per_batch = seq // tm
    return pl.pallas_call(
        _out_kernel,
        grid=(t // tm,),
        in_specs=[
            pl.BlockSpec((tm, kdim), lambda i: (i, 0)),
            pl.BlockSpec((kdim, d), lambda i: (0, 0), pipeline_mode=pl.Buffered(1)),
            pl.BlockSpec((tm, d), lambda i: (i, 0)),
            pl.BlockSpec((1, d), lambda i: (0, 0)),
            pl.BlockSpec((None, 1, d), lambda i: (i // per_batch, 0, gate_blk)),
        ],
        out_specs=pl.BlockSpec((tm, d), lambda i: (i, 0)),
        out_shape=jax.ShapeDtypeStruct((t, d), F32),
        compiler_params=pltpu.CompilerParams(
            dimension_semantics=("arbitrary",), vmem_limit_bytes=VMEM_LIMIT),
        name=name,
    )(y2, w, x2, g, mod3)


def _bias_kernel(tbl_ref, o_ref, base_scr):
    head = pl.program_id(0)
    nb, lq, _ = o_ref.shape
    width = nb * LANES + lq

    @pl.when(head == 0)
    def _():
        u = lax.broadcasted_iota(jnp.int32, (tbl_ref.shape[1], width), 1)
        u = jnp.where(u >= nb * LANES, u - width, u)
        idx = jnp.clip(PAD - u, -MAX_REL, MAX_REL) + MAX_REL
        c = lax.broadcasted_iota(jnp.int32, (tbl_ref.shape[1], width), 0)
        onehot = jnp.where(c == idx, 1.0, 0.0)
        base_scr[...] = jnp.dot(tbl_ref[...], onehot, preferred_element_type=F32,
                                precision=lax.Precision.HIGHEST)

    base = base_scr[pl.ds(head, 1), :]
    tile = jnp.broadcast_to(base, (lq, width))
    r = lax.broadcasted_iota(jnp.int32, (lq, width), 0)
    tile = pltpu.roll(tile, 0, 1, stride=1, stride_axis=0)
    uu = lax.broadcasted_iota(jnp.int32, (lq, width), 1)
    cq = r // CHUNK
    ck = uu // CHUNK - LEFT_CHUNKS
    visible = (ck <= cq) & (ck >= cq - LEFT_CHUNKS)
    tile = jnp.where(visible, tile * LOG2E, NEG)
    for j in range(nb):
        o_ref[j] = tile[:, j * LANES:(j + 1) * LANES]


def _bias_call(rel_table):
    nrel = rel_table.shape[1]
    ncols = -(-nrel // LANES) * LANES
    tbl = jnp.pad(rel_table, ((0, 0), (0, ncols - nrel)))
    return pl.pallas_call(
        _bias_kernel,
        grid=(B_HEADS,),
        in_specs=[pl.BlockSpec((B_HEADS, ncols), lambda h: (0, 0))],
        out_specs=pl.BlockSpec((None, ATTN_NB, ATTN_LQ, LANES), lambda h: (h, 0, 0, 0)),
        out_shape=jax.ShapeDtypeStruct((B_HEADS, ATTN_NB, ATTN_LQ, LANES), F32),
        scratch_shapes=[pltpu.VMEM((B_HEADS, ATTN_NB * LANES + ATTN_LQ), F32)],
        compiler_params=pltpu.CompilerParams(
            dimension_semantics=("arbitrary",), vmem_limit_bytes=VMEM_LIMIT),
        name="relpos_bias",
    )(tbl)


def _attn_kernel(q_ref, z_ref, k_ref, v_ref, wb_ref, y_ref):
    lq = q_ref.shape[0]
    wk = PAD + lq
    start = pl.program_id(2) * lq
    wstart = pl.multiple_of(jnp.maximum(start - PAD, 0), LANES)
    off = jnp.maximum(PAD - start, 0) // LANES
    for h in range(q_ref.shape[1] // B_DH):
        cols = slice(h * B_DH, (h + 1) * B_DH)
        kwin = k_ref[pl.ds(wstart, wk), cols]
        vwin = v_ref[pl.ds(wstart, wk), cols]
        s = lax.dot_general(q_ref[:, cols], kwin, (((1,), (1,)), ((), ())), preferred_element_type=F32)
        s = s + jnp.concatenate([wb_ref[h, off + j] for j in range(wk // LANES)], axis=1)
        m = jnp.max(s, axis=1, keepdims=True)
        p = jnp.exp2(s - m)
        inv = 1.0 / jnp.sum(p, axis=1, keepdims=True)
        o = jnp.dot(p.astype(BF16), vwin, preferred_element_type=F32) * inv
        y_ref[:, cols] = (o * _silu(z_ref[:, cols].astype(F32))).astype(y_ref.dtype)


def _attn_call(qzkv, wb):
    bn, seq, _ = qzkv.shape
    lq = ATTN_LQ
    gw = ATTN_HPG * B_DH
    ng = B_HEADS // ATTN_HPG
    resident = pl.Buffered(1)
    return pl.pallas_call(
        _attn_kernel,
        grid=(ng, bn, seq // lq),
        in_specs=[
            pl.BlockSpec((None, lq, gw), lambda g, b, i: (b, i, g)),
            pl.BlockSpec((None, lq, gw), lambda g, b, i: (b, i, ng + g)),
            pl.BlockSpec((None, seq, gw), lambda g, b, i: (b, 0, 2 * ng + g), pipeline_mode=resident),
            pl.BlockSpec((None, seq, gw), lambda g, b, i: (b, 0, 3 * ng + g), pipeline_mode=resident),
            pl.BlockSpec((ATTN_HPG, ATTN_NB, lq, LANES), lambda g, b, i: (g, 0, 0, 0), pipeline_mode=resident),
        ],
        out_specs=pl.BlockSpec((None, lq, gw), lambda g, b, i: (b, i, g)),
        out_shape=jax.ShapeDtypeStruct((bn, seq, B_WIDTH), BF16),
        compiler_params=pltpu.CompilerParams(
            dimension_semantics=("arbitrary", "arbitrary", "arbitrary"), vmem_limit_bytes=VMEM_LIMIT),
        name="band_attention",
    )(qzkv, qzkv, qzkv, qzkv, wb)


def kernel(x, c, ada_w, ada_b, g_pre, g_post, a_w_in, a_conv_w, a_conv_b, a_gate_b, a_g_head, a_w_out,
           kv_ada_w, kv_ada_b, kv_g, kv_w, b_w_in, b_rel, b_w_out):
    bn, seq, d = x.shape
    t = bn * seq
    assert bn <= SUBLANES and seq % 1024 == 0 and seq >= PAD + ATTN_LQ
    x2 = x.reshape(t, d)
    c8 = jnp.pad(c.astype(F32), ((0, SUBLANES - bn), (0, 0)))

    mod0 = _mod_call(c8, ada_w, ada_b, layer=0).reshape(SUBLANES, 1, 3 * d)
    n_main = 2 * A_QK + 3 * A_INNER
    w_main_t, w_gate_t = _cast_w_in_call(a_w_in[0].T, n_main)
    proj, gates = _proj_call(x2, [(g_pre[0].reshape(1, d), mod0)], w_main_t, w_gate_t, seq, "mlstm_in_proj",
                             tn=2048, trans_w=True)
    gates_row = gates[:, :2 * A_HEADS].reshape(bn, seq, 2, A_HEADS).transpose(0, 2, 3, 1)
    gates_row = gates_row.reshape(bn, 2, A_HEADS, 1, seq)
    y0 = _mlstm_call(proj.reshape(bn, seq, n_main), gates_row, a_gate_b[0], a_conv_w[0],
                     a_conv_b[0].reshape(1, 2 * A_QK), a_g_head[0].reshape(1, A_INNER))
    x2 = _out_call(y0.reshape(t, A_INNER), a_w_out[0].astype(BF16), x2, g_post[0].reshape(1, d),
                   mod0, 2, seq, "mlstm_out_proj")

    mod1 = _mod_call(c8, ada_w, ada_b, layer=1).reshape(SUBLANES, 1, 3 * d)
    kmod = _mod_call(c8, kv_ada_w, kv_ada_b).reshape(SUBLANES, 1, 2 * d)
    w_qzkv = _cast_pair_call(b_w_in[0], kv_w)
    qzkv = _proj_call(x2, [(g_pre[1].reshape(1, d), mod1), (kv_g.reshape(1, d), kmod)], w_qzkv, None, seq,
                      "attn_qzkv_proj", tn=1024, n1=2 * B_WIDTH,
                      lead_cols=B_WIDTH, lead_scale=(B_DH ** -0.5) * LOG2E)
    wb = _bias_call(b_rel[0])
    y1 = _attn_call(qzkv.reshape(bn, seq, 4 * B_WIDTH), wb)
    x2 = _out_call(y1.reshape(t, B_WIDTH), b_w_out[0].astype(BF16), x2, g_post[1].reshape(1, d),
                   mod1, 2, seq, "attn_out_proj")
    return x2.reshape(bn, seq, d)
```

```python
import functools

import jax
import jax.numpy as jnp
from jax import lax
from jax.experimental import pallas as pl
from jax.experimental.pallas import tpu as pltpu

F32 = jnp.float32
BF16 = jnp.bfloat16

EPS = 1e-6
CHUNK = 64
A_HEADS = 8
A_DK = 256
A_DV = 512
A_QK = A_HEADS * A_DK
A_INNER = A_HEADS * A_DV
CONV_W = 4
B_HEADS = 16
B_DH = 128
B_WIDTH = B_HEADS * B_DH
LEFT_CHUNKS = 8
PAD = LEFT_CHUNKS * CHUNK
MAX_REL = 128
NEG = -1e30

LANES = 128
SUBLANES = 8

LOG2E = 1.4426950408889634

MLSTM_CHUNK = 256
MLSTM_HPG = 8
ATTN_LQ = 256
ATTN_HPG = 8
ATTN_WK = PAD + ATTN_LQ
ATTN_NB = (PAD + ATTN_WK) // LANES

VMEM_LIMIT = 56 * 1024 * 1024


def _sigmoid(v):
    return 0.5 * jnp.tanh(0.5 * v) + 0.5


def _silu(v):
    hv = 0.5 * v
    return hv * jnp.tanh(hv) + hv


def _log_sigmoid(v):
    return -(jnp.maximum(-v, 0.0) + jnp.log1p(jnp.exp(-jnp.abs(v))))


def _mod_kernel(c_ref, w_ref, b_ref, o_ref):
    sc = _silu(c_ref[...])
    o_ref[...] = jnp.dot(sc.astype(BF16), w_ref[...].astype(BF16),
                         preferred_element_type=F32) + b_ref[...]


def _mod_call(c8, w, b, layer=None):
    d = c8.shape[1]
    n = w.shape[-1]
    tn = 1024
    if layer is None:
        w_spec = pl.BlockSpec((d, tn), lambda j: (0, j))
        b_spec = pl.BlockSpec((1, tn), lambda j: (0, j))
        b2 = b.reshape(1, n)
    else:
        w_spec = pl.BlockSpec((None, d, tn), lambda j: (layer, 0, j))
        b_spec = pl.BlockSpec((None, 1, tn), lambda j: (layer, 0, j))
        b2 = b.reshape(b.shape[0], 1, n)
    return pl.pallas_call(
        _mod_kernel,
        grid=(n // tn,),
        in_specs=[pl.BlockSpec((SUBLANES, d), lambda j: (0, 0)), w_spec, b_spec],
        out_specs=pl.BlockSpec((SUBLANES, tn), lambda j: (0, j)),
        out_shape=jax.ShapeDtypeStruct((SUBLANES, n), F32),
        compiler_params=pltpu.CompilerParams(
            dimension_semantics=("arbitrary",), vmem_limit_bytes=VMEM_LIMIT),
        name="adaln_mod",
    )(c8, w, b2)


def _cast_kernel(w_ref, wg_ref, o_ref, og_ref):
    o_ref[...] = w_ref[...].astype(BF16)

    @pl.when(pl.program_id(0) == 0)
    def _():
        r = lax.broadcasted_iota(jnp.int32, wg_ref.shape, 0)
        og_ref[...] = jnp.where(r < 2 * A_HEADS, wg_ref[...], 0.0).astype(BF16)


def _cast_w_in_call(w_t, n_main):
    _, d = w_t.shape
    tn = 1024
    return pl.pallas_call(
        _cast_kernel,
        grid=(n_main // tn,),
        in_specs=[
            pl.BlockSpec((tn, d), lambda j: (j, 0)),
            pl.BlockSpec((LANES, d), lambda j: (n_main // LANES, 0)),
        ],
        out_specs=[
            pl.BlockSpec((tn, d), lambda j: (j, 0)),
            pl.BlockSpec((LANES, d), lambda j: (0, 0)),
        ],
        out_shape=[jax.ShapeDtypeStruct((n_main, d), BF16), jax.ShapeDtypeStruct((LANES, d), BF16)],
        compiler_params=pltpu.CompilerParams(
            dimension_semantics=("arbitrary",), vmem_limit_bytes=VMEM_LIMIT),
        name="cast_w_in",
    )(w_t, w_t)


def _cast_pair_kernel(w1_ref, w2_ref, o_ref, *, n1_tiles):
    @pl.when(pl.program_id(0) < n1_tiles)
    def _():
        o_ref[...] = w1_ref[...].astype(BF16)

    @pl.when(pl.program_id(0) >= n1_tiles)
    def _():
        o_ref[...] = w2_ref[...].astype(BF16)


def _cast_pair_call(w1, w2):
    d, n1 = w1.shape
    n2 = w2.shape[1]
    tn = 512
    n1_tiles = n1 // tn
    return pl.pallas_call(
        functools.partial(_cast_pair_kernel, n1_tiles=n1_tiles),
        grid=((n1 + n2) // tn,),
        in_specs=[
            pl.BlockSpec((d, tn), lambda j: (0, jnp.minimum(j, n1_tiles - 1))),
            pl.BlockSpec((d, tn), lambda j: (0, jnp.maximum(j - n1_tiles, 0))),
        ],
        out_specs=pl.BlockSpec((d, tn), lambda j: (0, j)),
        out_shape=jax.ShapeDtypeStruct((d, n1 + n2), BF16),
        compiler_params=pltpu.CompilerParams(
            dimension_semantics=("arbitrary",), vmem_limit_bytes=VMEM_LIMIT),
        name="cast_w_qzkv",
    )(w1, w2)


_NT =(((1,), (1,)), ((), ()))


def _modulated(xn, g_ref, shift_ref, scale_ref):
    return ((xn * g_ref[...]) * (1.0 + scale_ref[...]) + shift_ref[...]).astype(BF16)


def _proj_kernel(*refs, n_mod, trans_w, with_gates, n1_tiles, lead_tiles, lead_scale, rows):
    x_ref = refs[0]
    mod_refs = refs[1:1 + 3 * n_mod]
    rest = list(refs[1 + 3 * n_mod:])
    w_ref = rest.pop(0)
    wg_ref = rest.pop(0) if with_gates else None
    o_ref = rest.pop(0)
    og_ref = rest.pop(0) if with_gates else None
    h_scr = rest.pop(0)
    phase = pl.program_id(0)
    j = pl.program_id(1)
    slot = phase % 2

    def normalise_rows():
        r0 = pl.multiple_of(j * rows, rows)
        x = x_ref[pl.ds(r0, rows), :]
        xn = x * lax.rsqrt(jnp.mean(x * x, axis=-1, keepdims=True) + EPS)
        for k in range(n_mod):
            hb = _modulated(xn, *mod_refs[3 * k:3 * k + 3])
            h_scr[slot, k, pl.ds(r0, rows), :] = hb
            if with_gates and k == 0:
                og_ref[pl.ds(r0, rows), :] = lax.dot_general(hb, wg_ref[...], _NT, preferred_element_type=F32)

    def project():
        k = jnp.where(j >= n1_tiles, 1, 0) if n_mod == 2 else 0
        h = h_scr[1 - slot, k]
        if trans_w:
            acc = lax.dot_general(h, w_ref[...], _NT, preferred_element_type=F32)
        else:
            acc = jnp.dot(h, w_ref[...], preferred_element_type=F32)
        if lead_tiles:
            acc = acc * jnp.where(j < lead_tiles, lead_scale, 1.0)
        o_ref[...] = acc.astype(o_ref.dtype)

    @pl.when(phase == 0)
    def _():
        normalise_rows()

    @pl.when(phase > 0)
    def _():
        project()
        normalise_rows()


def _proj_call(x2, mods, w, wg, seq, name, *, tn, trans_w=False, n1=None, lead_cols=0, lead_scale=1.0):
    t, d = x2.shape
    n = w.shape[0] if trans_w else w.shape[1]
    tm = 1024
    nt, nj = t // tm, n // tn
    per_batch = seq // tm
    n_mod = len(mods)
    n1 = n if n1 is None else n1
    assert lead_cols % tn == 0 and n1 % tn == 0 and n % tn == 0 and tm % nj == 0

    def row_tile(p):
        return jnp.minimum(p, nt - 1)

    vec = lambda blk: pl.BlockSpec((None, 1, d), lambda p, j: (row_tile(p) // per_batch, 0, blk))
    in_specs = [pl.BlockSpec((tm, d), lambda p, j: (row_tile(p), 0))]
    args = [x2]
    for g, mod3 in mods:
        in_specs += [pl.BlockSpec((1, d), lambda p, j: (0, 0)), vec(0), vec(1)]
        args += [g, mod3, mod3]
    in_specs.append(pl.BlockSpec((tn, d), lambda p, j: (j, 0)) if trans_w
                    else pl.BlockSpec((d, tn), lambda p, j: (0, j)))
    args.append(w)
    out_specs = pl.BlockSpec((tm, tn), lambda p, j: (jnp.maximum(p - 1, 0), jnp.where(p == 0, 0, j)))
    out_shape = jax.ShapeDtypeStruct((t, n), BF16)
    if wg is not None:
        in_specs.append(pl.BlockSpec((LANES, d), lambda p, j: (0, 0)))
        args.append(wg)
        out_specs = [out_specs, pl.BlockSpec((tm, LANES), lambda p, j: (row_tile(p), 0))]
        out_shape = [out_shape, jax.ShapeDtypeStruct((t, LANES), F32)]
    return pl.pallas_call(
        functools.partial(_proj_kernel, n_mod=n_mod, trans_w=trans_w, with_gates=wg is not None,
                          n1_tiles=n1 // tn, lead_tiles=lead_cols // tn, lead_scale=lead_scale, rows=tm // nj),
        grid=(nt + 1, nj),
        in_specs=in_specs,
        out_specs=out_specs,
        out_shape=out_shape,
        scratch_shapes=[pltpu.VMEM((2, n_mod, tm, d), BF16)],
        compiler_params=pltpu.CompilerParams(
            dimension_semantics=("arbitrary", "arbitrary"), vmem_limit_bytes=VMEM_LIMIT),
        name=name,
    )(*args)


def _mlstm_kernel(gb_ref, sh_ref, q_ref, k_ref, v_ref, o_ref, z_ref, gr_ref, cwq_ref, cwk_ref,
                  cbq_ref, cbk_ref, gh_ref, y_ref, c_scr, n_scr, m_scr, hq_scr, hk_scr):
    group = pl.program_id(1)
    L = q_ref.shape[0]
    hp = q_ref.shape[1] // A_DK

    @pl.when(pl.program_id(2) == 0)
    def _():
        c_scr[...] = jnp.zeros_like(c_scr)
        n_scr[...] = jnp.zeros_like(n_scr)
        m_scr[...] = jnp.zeros_like(m_scr)
        hq_scr[...] = jnp.zeros_like(hq_scr)
        hk_scr[...] = jnp.zeros_like(hk_scr)

    row8 = lax.broadcasted_iota(jnp.int32, (SUBLANES, LANES), 0)

    def conv_silu(u_ref, halo_scr, w_ref, b_ref):
        blocks = []
        for cb in range(u_ref.shape[1] // LANES):
            cols = slice(cb * LANES, (cb + 1) * LANES)
            ub = u_ref[:, cols]
            u = ub.astype(F32)
            shifted = jnp.dot(sh_ref[...], ub, preferred_element_type=F32)
            halo = halo_scr[:, cols]
            acc = b_ref[:, cols]
            for j in range(CONV_W - 1):
                s = CONV_W - 1 - j
                blk = shifted[(s - 1) * L:s * L]
                fix = jnp.where(row8 < s, pltpu.roll(halo, s, 0), 0.0)
                blk = jnp.concatenate([blk[:SUBLANES] + fix, blk[SUBLANES:]], axis=0)
                acc = acc + blk * w_ref[j:j + 1, cols]
            acc = acc + u * w_ref[CONV_W - 1:CONV_W, cols]
            halo_scr[:, cols] = u[L - SUBLANES:]
            blocks.append(_silu(acc))
        bph = A_DK // LANES
        return [jnp.concatenate(blocks[h * bph:(h + 1) * bph], axis=1) for h in range(len(blocks) // bph)]

    q_heads = conv_silu(q_ref, hq_scr, cwq_ref, cbq_ref)
    k_heads = [kh * (A_DK ** -0.5) for kh in conv_silu(k_ref, hk_scr, cwk_ref, cbk_ref)]

    row = lax.broadcasted_iota(jnp.int32, (L, L), 0)
    col = lax.broadcasted_iota(jnp.int32, (L, L), 1)
    causal = col <= row
    eye = col == row

    for h in range(hp):
        head = group * hp + h
        q = q_heads[h]
        k = k_heads[h]
        vcols = slice(h * A_DV, (h + 1) * A_DV)
        vb = v_ref[:, vcols]

        li_row = gr_ref[0, h] + gb_ref[head]
        lf_row = _log_sigmoid(gr_ref[1, h] + gb_ref[A_HEADS + head])

        b_col = jnp.sum(jnp.where(causal, lf_row, 0.0), axis=1, keepdims=True)
        b_row = jnp.sum(jnp.where(eye, b_col, 0.0), axis=0, keepdims=True)
        a_row = li_row - b_row
        dm = jnp.where(causal, b_col + a_row, NEG)
        m_prev = m_scr[h]
        inter = b_col + m_prev
        m_t = jnp.maximum(inter, jnp.max(dm, axis=1, keepdims=True))
        w_intra = jnp.exp(dm - m_t)
        w_inter = jnp.exp(inter - m_t)

        qb = q.astype(BF16)
        kb = k.astype(BF16)
        s = lax.dot_general(qb, kb, (((1,), (1,)), ((), ())), preferred_element_type=F32) * w_intra
        c_old = c_scr[h]
        n_old = n_scr[h]
        num = (w_inter * jnp.dot(qb, c_old.astype(BF16), preferred_element_type=F32)
               + jnp.dot(s.astype(BF16), vb, preferred_element_type=F32))
        den = (w_inter * jnp.sum(q * n_old, axis=1, keepdims=True)
               + jnp.sum(s, axis=1, keepdims=True))
        r = 1.0 / jnp.maximum(jnp.abs(den), jnp.exp(-m_t))
        rn = r * lax.rsqrt(r * r * jnp.mean(num * num, axis=-1, keepdims=True) + EPS)
        hn = num * rn * gh_ref[:, vcols]
        gate = _sigmoid(o_ref[:, vcols]) * _silu(z_ref[:, vcols])
        y = hn * gate.astype(F32)
        y_ref[:, vcols] = y.astype(y_ref.dtype)

        b_last = b_col[L - 1:L, :]
        g_row = b_last + a_row
        m_new = jnp.maximum(b_last + m_prev, jnp.max(g_row, axis=1, keepdims=True))
        decay = jnp.exp(b_last + m_prev - m_new)
        ws_row = jnp.exp(g_row - m_new)
        ws_col = jnp.sum(jnp.where(eye, ws_row, 0.0), axis=1, keepdims=True)
        kw = k * ws_col
        c_scr[h] = decay * c_old + lax.dot_general(
            kw.astype(BF16), vb, (((0,), (0,)), ((), ())), preferred_element_type=F32)
        n_scr[h] = decay * n_old + jnp.sum(kw, axis=0, keepdims=True)
        m_scr[h] = m_new


def _shift_matrix(L):
    t = jnp.arange(L)
    blocks = [(t[:, None] - s == t[None, :]) for s in range(1, CONV_W)]
    return jnp.concatenate(blocks, axis=0).astype(BF16)


def _mlstm_call(proj, gates_row, gate_b, conv_w, conv_b, g_head):
    bn, seq, _ = proj.shape
    L = MLSTM_CHUNK
    hp = MLSTM_HPG
    ng = A_HEADS // hp
    qw, vw = hp * A_DK, hp * A_DV
    v0 = 2 * A_QK // vw
    return pl.pallas_call(
        _mlstm_kernel,
        grid=(bn, ng, seq // L),
        in_specs=[
            pl.BlockSpec(memory_space=pltpu.SMEM),
            pl.BlockSpec(((CONV_W - 1) * L, L), lambda b, g, i: (0, 0)),
            pl.BlockSpec((None, L, qw), lambda b, g, i: (b, i, g)),
            pl.BlockSpec((None, L, qw), lambda b, g, i: (b, i, ng + g)),
            pl.BlockSpec((None, L, vw), lambda b, g, i: (b, i, v0 + g)),
            pl.BlockSpec((None, L, vw), lambda b, g, i: (b, i, v0 + ng + g)),
            pl.BlockSpec((None, L, vw), lambda b, g, i: (b, i, v0 + 2 * ng + g)),
            pl.BlockSpec((None, 2, hp, 1, L), lambda b, g, i: (b, 0, g, 0, i)),
            pl.BlockSpec((CONV_W, qw), lambda b, g, i: (0, g)),
            pl.BlockSpec((CONV_W, qw), lambda b, g, i: (0, ng + g)),
            pl.BlockSpec((1, qw), lambda b, g, i: (0, g)),
            pl.BlockSpec((1, qw), lambda b, g, i: (0, ng + g)),
            pl.BlockSpec((1, vw), lambda b, g, i: (0, g)),
        ],
        out_specs=pl.BlockSpec((None, L, vw), lambda b, g, i: (b, i, g)),
        out_shape=jax.ShapeDtypeStruct((bn, seq, A_INNER), BF16),
        scratch_shapes=[
            pltpu.VMEM((hp, A_DK, A_DV), F32),
            pltpu.VMEM((hp, 1, A_DK), F32),
            pltpu.VMEM((hp, 1, 1), F32),
            pltpu.VMEM((SUBLANES, qw), F32),
            pltpu.VMEM((SUBLANES, qw), F32),
        ],
        compiler_params=pltpu.CompilerParams(
            dimension_semantics=("arbitrary", "arbitrary", "arbitrary"), vmem_limit_bytes=VMEM_LIMIT),
        name="mlstm_core",
    )(gate_b, _shift_matrix(L), proj, proj, proj, proj, proj, gates_row, conv_w, conv_w, conv_b, conv_b, g_head)


def _out_kernel(y_ref, w_ref, x_ref, g_ref, gate_ref, o_ref):
    y = jnp.dot(y_ref[...], w_ref[...], preferred_element_type=F32)
    yn = y * lax.rsqrt(jnp.mean(y * y, axis=-1, keepdims=True) + EPS) * g_ref[...]
    o_ref[...] = x_ref[...] + gate_ref[...] * yn


def _out_call(y2, w, x2, g, mod3, gate_blk, seq, name):
    t, kdim = y2.shape
    d = w.shape[1]
    tm = 512
    per_batch = seq // tm
    return pl.pallas_call(
        _out_kernel,
        grid=(t // tm,),
        in_specs=[
            pl.BlockSpec((tm, kdim), lambda i: (i, 0)),
            pl.BlockSpec((kdim, d), lambda i: (0, 0), pipeline_mode=pl.Buffered(1)),
            pl.BlockSpec((tm, d), lambda i: (i, 0)),
            pl.BlockSpec((1, d), lambda i: (0, 0)),
            pl.BlockSpec((None, 1, d), lambda i: (i // per_batch, 0, gate_blk)),
        ],
        out_specs=pl.BlockSpec((tm, d), lambda i: (i, 0)),
        out_shape=jax.ShapeDtypeStruct((t, d), F32),
        compiler_params=pltpu.CompilerParams(
            dimension_semantics=("arbitrary",), vmem_limit_bytes=VMEM_LIMIT),
        name=name,
    )(y2, w, x2, g, mod3)


def _bias_kernel(tbl_ref, o_ref, base_scr):
    head = pl.program_id(0)
    nb, lq, _ = o_ref.shape
    width = nb * LANES + lq

    @pl.when(head == 0)
    def _():
        u = lax.broadcasted_iota(jnp.int32, (tbl_ref.shape[1], width), 1)
        u = jnp.where(u >= nb * LANES, u - width, u)
        idx = jnp.clip(PAD - u, -MAX_REL, MAX_REL) + MAX_REL
        c = lax.broadcasted_iota(jnp.int32, (tbl_ref.shape[1], width), 0)
        onehot = jnp.where(c == idx, 1.0, 0.0)
        base_scr[...] = jnp.dot(tbl_ref[...], onehot, preferred_element_type=F32,
                                precision=lax.Precision.HIGHEST)

    base = base_scr[pl.ds(head, 1), :]
    tile = jnp.broadcast_to(base, (lq, width))
    r = lax.broadcasted_iota(jnp.int32, (lq, width), 0)
    tile = pltpu.roll(tile, 0, 1, stride=1, stride_axis=0)
    uu = lax.broadcasted_iota(jnp.int32, (lq, width), 1)
    cq = r // CHUNK
    ck = uu // CHUNK - LEFT_CHUNKS
    visible = (ck <= cq) & (ck >= cq - LEFT_CHUNKS)
    tile = jnp.where(visible, tile * LOG2E, NEG)
    for j in range(nb):
        o_ref[j] = tile[:, j * LANES:(j + 1) * LANES]


def _bias_call(rel_table):
    nrel = rel_table.shape[1]
    ncols = -(-nrel // LANES) * LANES
    tbl = jnp.pad(rel_table, ((0, 0), (0, ncols - nrel)))
    return pl.pallas_call(
        _bias_kernel,
        grid=(B_HEADS,),
        in_specs=[pl.BlockSpec((B_HEADS, ncols), lambda h: (0, 0))],
        out_specs=pl.BlockSpec((None, ATTN_NB, ATTN_LQ, LANES), lambda h: (h, 0, 0, 0)),
        out_shape=jax.ShapeDtypeStruct((B_HEADS, ATTN_NB, ATTN_LQ, LANES), F32),
        scratch_shapes=[pltpu.VMEM((B_HEADS, ATTN_NB * LANES + ATTN_LQ), F32)],
        compiler_params=pltpu.CompilerParams(
            dimension_semantics=("arbitrary",), vmem_limit_bytes=VMEM_LIMIT),
        name="relpos_bias",
    )(tbl)


def _attn_kernel(q_ref, z_ref, k_ref, v_ref, wb_ref, y_ref):
    lq = q_ref.shape[0]
    wk = PAD + lq
    start = pl.program_id(2) * lq
    wstart = pl.multiple_of(jnp.maximum(start - PAD, 0), LANES)
    off = jnp.maximum(PAD - start, 0) // LANES
    for h in range(q_ref.shape[1] // B_DH):
        cols = slice(h * B_DH, (h + 1) * B_DH)
        kwin = k_ref[pl.ds(wstart, wk), cols]
        vwin = v_ref[pl.ds(wstart, wk), cols]
        s = lax.dot_general(q_ref[:, cols], kwin, (((1,), (1,)), ((), ())), preferred_element_type=F32)
        s = s + jnp.concatenate([wb_ref[h, off + j] for j in range(wk // LANES)], axis=1)
        m = jnp.max(s, axis=1, keepdims=True)
        p = jnp.exp2(s - m)
        inv = 1.0 / jnp.sum(p, axis=1, keepdims=True)
        o = jnp.dot(p.astype(BF16), vwin, preferred_element_type=F32) * inv
        y_ref[:, cols] = (o * _silu(z_ref[:, cols].astype(F32))).astype(y_ref.dtype)


def _attn_call(qzkv, wb):
    bn, seq, _ = qzkv.shape
    lq = ATTN_LQ
    gw = ATTN_HPG * B_DH
    ng = B_HEADS // ATTN_HPG
    resident = pl.Buffered(1)
    return pl.pallas_call(
        _attn_kernel,
        grid=(ng, bn, seq // lq),
        in_specs=[
            pl.BlockSpec((None, lq, gw), lambda g, b, i: (b, i, g)),
            pl.BlockSpec((None, lq, gw), lambda g, b, i: (b, i, ng + g)),
            pl.BlockSpec((None, seq, gw), lambda g, b, i: (b, 0, 2 * ng + g)),
            pl.BlockSpec((None, seq, gw), lambda g, b, i: (b, 0, 3 * ng + g)),
            pl.BlockSpec((ATTN_HPG, ATTN_NB, lq, LANES), lambda g, b, i: (g, 0, 0, 0), pipeline_mode=resident),
        ],
        out_specs=pl.BlockSpec((None, lq, gw), lambda g, b, i: (b, i, g)),
        out_shape=jax.ShapeDtypeStruct((bn, seq, B_WIDTH), BF16),
        compiler_params=pltpu.CompilerParams(
            dimension_semantics=("arbitrary", "arbitrary", "arbitrary"), vmem_limit_bytes=VMEM_LIMIT),
        name="band_attention",
    )(qzkv, qzkv, qzkv, qzkv, wb)


def kernel(x, c, ada_w, ada_b, g_pre, g_post, a_w_in, a_conv_w, a_conv_b, a_gate_b, a_g_head, a_w_out,
           kv_ada_w, kv_ada_b, kv_g, kv_w, b_w_in, b_rel, b_w_out):
    bn, seq, d = x.shape
    t = bn * seq
    assert bn <= SUBLANES and seq % 1024 == 0 and seq >= PAD + ATTN_LQ
    x2 = x.reshape(t, d)
    c8 = jnp.pad(c.astype(F32), ((0, SUBLANES - bn), (0, 0)))

    mod0 = _mod_call(c8, ada_w, ada_b, layer=0).reshape(SUBLANES, 1, 3 * d)
    n_main = 2 * A_QK + 3 * A_INNER
    w_main_t, w_gate_t = _cast_w_in_call(a_w_in[0].T, n_main)
    proj, gates = _proj_call(x2, [(g_pre[0].reshape(1, d), mod0)], w_main_t, w_gate_t, seq, "mlstm_in_proj",
                             tn=2048, trans_w=True)
    gates_row = gates[:, :2 * A_HEADS].reshape(bn, seq, 2, A_HEADS).transpose(0, 2, 3, 1)
    gates_row = gates_row.reshape(bn, 2, A_HEADS, 1, seq)
    y0 = _mlstm_call(proj.reshape(bn, seq, n_main), gates_row, a_gate_b[0], a_conv_w[0],
                     a_conv_b[0].reshape(1, 2 * A_QK), a_g_head[0].reshape(1, A_INNER))
    x2 = _out_call(y0.reshape(t, A_INNER), a_w_out[0].astype(BF16), x2, g_post[0].reshape(1, d),
                   mod0, 2, seq, "mlstm_out_proj")

    mod1 = _mod_call(c8, ada_w, ada_b, layer=1).reshape(SUBLANES, 1, 3 * d)
    kmod = _mod_call(c8, kv_ada_w, kv_ada_b).reshape(SUBLANES, 1, 2 * d)
    w_qzkv = _cast_pair_call(b_w_in[0], kv_w)
    qzkv = _proj_call(x2, [(g_pre[1].reshape(1, d), mod1), (kv_g.reshape(1, d), kmod)], w_qzkv, None, seq,
                      "attn_qzkv_proj", tn=1024, n1=2 * B_WIDTH,
                      lead_cols=B_WIDTH, lead_scale=(B_DH ** -0.5) * LOG2E)
    wb = _bias_call(b_rel[0])
    y1 = _attn_call(qzkv.reshape(bn, seq, 4 * B_WIDTH), wb)
    x2 = _out_call(y1.reshape(t, B_WIDTH), b_w_out[0].astype(BF16), x2, g_post[1].reshape(1, d),
                   mod1, 2, seq, "attn_out_proj")
    return x2.reshape(bn, seq, d)
```

```python
import functools

import jax
import jax.numpy as jnp
from jax import lax
from jax.experimental import pallas as pl
from jax.experimental.pallas import tpu as pltpu

F32 = jnp.float32
BF16 = jnp.bfloat16

EPS = 1e-6
CHUNK = 64
A_HEADS = 8
A_DK = 256
A_DV = 512
A_QK = A_HEADS * A_DK
A_INNER = A_HEADS * A_DV
CONV_W = 4
B_HEADS = 16
B_DH = 128
B_WIDTH = B_HEADS * B_DH
LEFT_CHUNKS = 8
PAD = LEFT_CHUNKS * CHUNK
MAX_REL = 128
NEG = -1e30

LANES = 128
SUBLANES = 8

LOG2E = 1.4426950408889634

MLSTM_CHUNK = 256
MLSTM_HPG = 8
ATTN_LQ = 256
ATTN_HPG = 8
ATTN_WK = PAD + ATTN_LQ
ATTN_NB = (PAD + ATTN_WK) // LANES

VMEM_LIMIT = 56 * 1024 * 1024


def _sigmoid(v):
    return 0.5 * jnp.tanh(0.5 * v) + 0.5


def _silu(v):
    hv = 0.5 * v
    return hv * jnp.tanh(hv) + hv


def _log_sigmoid(v):
    return -(jnp.maximum(-v, 0.0) + jnp.log1p(jnp.exp(-jnp.abs(v))))


def _mod_kernel(c_ref, w_ref, b_ref, o_ref):
    sc = _silu(c_ref[...])
    o_ref[...] = jnp.dot(sc.astype(BF16), w_ref[...].astype(BF16),
                         preferred_element_type=F32) + b_ref[...]


def _mod_call(c8, w, b, layer=None):
    d = c8.shape[1]
    n = w.shape[-1]
    tn = 1024
    if layer is None:
        w_spec = pl.BlockSpec((d, tn), lambda j: (0, j))
        b_spec = pl.BlockSpec((1, tn), lambda j: (0, j))
        b2 = b.reshape(1, n)
    else:
        w_spec = pl.BlockSpec((None, d, tn), lambda j: (layer, 0, j))
        b_spec = pl.BlockSpec((None, 1, tn), lambda j: (layer, 0, j))
        b2 = b.reshape(b.shape[0], 1, n)
    return pl.pallas_call(
        _mod_kernel,
        grid=(n // tn,),
        in_specs=[pl.BlockSpec((SUBLANES, d), lambda j: (0, 0)), w_spec, b_spec],
        out_specs=pl.BlockSpec((SUBLANES, tn), lambda j: (0, j)),
        out_shape=jax.ShapeDtypeStruct((SUBLANES, n), F32),
        compiler_params=pltpu.CompilerParams(
            dimension_semantics=("arbitrary",), vmem_limit_bytes=VMEM_LIMIT),
        name="adaln_mod",
    )(c8, w, b2)


def _cast_kernel(w_ref, wg_ref, o_ref, og_ref):
    o_ref[...] = w_ref[...].astype(BF16)

    @pl.when(pl.program_id(0) == 0)
    def _():
        r = lax.broadcasted_iota(jnp.int32, wg_ref.shape, 0)
        og_ref[...] = jnp.where(r < 2 * A_HEADS, wg_ref[...], 0.0).astype(BF16)


def _cast_w_in_call(w_t, n_main):
    _, d = w_t.shape
    tn = 1024
    return pl.pallas_call(
        _cast_kernel,
        grid=(n_main // tn,),
        in_specs=[
            pl.BlockSpec((tn, d), lambda j: (j, 0)),
            pl.BlockSpec((LANES, d), lambda j: (n_main // LANES, 0)),
        ],
        out_specs=[
            pl.BlockSpec((tn, d), lambda j: (j, 0)),
            pl.BlockSpec((LANES, d), lambda j: (0, 0)),
        ],
        out_shape=[jax.ShapeDtypeStruct((n_main, d), BF16), jax.ShapeDtypeStruct((LANES, d), BF16)],
        compiler_params=pltpu.CompilerParams(
            dimension_semantics=("arbitrary",), vmem_limit_bytes=VMEM_LIMIT),
        name="cast_w_in",
    )(w_t, w_t)


def _cast_pair_kernel(w1_ref, w2_ref, o_ref, *, n1_tiles):
    @pl.when(pl.program_id(0) < n1_tiles)
    def _():
        o_ref[...] = w1_ref[...].astype(BF16)

    @pl.when(pl.program_id(0) >= n1_tiles)
    def _():
        o_ref[...] = w2_ref[...].astype(BF16)


def _cast_pair_call(w1, w2):
    d, n1 = w1.shape
    n2 = w2.shape[1]
    tn = 512
    n1_tiles = n1 // tn
    return pl.pallas_call(
        functools.partial(_cast_pair_kernel, n1_tiles=n1_tiles),
        grid=((n1 + n2) // tn,),
        in_specs=[
            pl.BlockSpec((d, tn), lambda j: (0, jnp.minimum(j, n1_tiles - 1))),
            pl.BlockSpec((d, tn), lambda j: (0, jnp.maximum(j - n1_tiles, 0))),
        ],
        out_specs=pl.BlockSpec((d, tn), lambda j: (0, j)),
        out_shape=jax.ShapeDtypeStruct((d, n1 + n2), BF16),
        compiler_params=pltpu.CompilerParams(
            dimension_semantics=("arbitrary",), vmem_limit_bytes=VMEM_LIMIT),
        name="cast_w_qzkv",
    )(w1, w2)


_NT =(((1,), (1,)), ((), ()))


def _modulated(xn, g_ref, shift_ref, scale_ref):
    return ((xn * g_ref[...]) * (1.0 + scale_ref[...]) + shift_ref[...]).astype(BF16)


def _proj_kernel(*refs, n_mod, trans_w, with_gates, n1_tiles, lead_tiles, lead_scale):
    x_ref = refs[0]
    mod_refs = refs[1:1 + 3 * n_mod]
    rest = list(refs[1 + 3 * n_mod:])
    w_ref = rest.pop(0)
    wg_ref = rest.pop(0) if with_gates else None
    o_ref = rest.pop(0)
    og_ref = rest.pop(0) if with_gates else None
    h_scr = rest.pop(0)
    j = pl.program_id(1)

    def project(h):
        if trans_w:
            acc = lax.dot_general(h, w_ref[...], _NT, preferred_element_type=F32)
        else:
            acc = jnp.dot(h, w_ref[...], preferred_element_type=F32)
        if lead_tiles:
            acc = acc * jnp.where(j < lead_tiles, lead_scale, 1.0)
        o_ref[...] = acc.astype(o_ref.dtype)

    @pl.when(j == 0)
    def _():
        x = x_ref[...]
        xn = x * lax.rsqrt(jnp.mean(x * x, axis=-1, keepdims=True) + EPS)
        hs = []
        for k in range(n_mod):
            hb = _modulated(xn, *mod_refs[3 * k:3 * k + 3])
            h_scr[k] = hb
            hs.append(hb)
        if with_gates:
            og_ref[...] = lax.dot_general(hs[0], wg_ref[...], _NT, preferred_element_type=F32)
        project(hs[0])

    @pl.when(j > 0)
    def _():
        project(h_scr[jnp.where(j >= n1_tiles, 1, 0) if n_mod == 2 else 0])


def _proj_call(x2, mods, w, wg, seq, name, *, tn, trans_w=False, n1=None, lead_cols=0, lead_scale=1.0):
    t, d = x2.shape
    n = w.shape[0] if trans_w else w.shape[1]
    tm = 1024
    nt, nj = t // tm, n // tn
    per_batch = seq // tm
    n_mod = len(mods)
    n1 = n if n1 is None else n1
    assert lead_cols % tn == 0 and n1 % tn == 0 and n % tn == 0 and n1 >= tn

    vec = lambda blk: pl.BlockSpec((None, 1, d), lambda i, j: (i // per_batch, 0, blk))
    in_specs = [pl.BlockSpec((tm, d), lambda i, j: (i, 0))]
    args = [x2]
    for g, mod3 in mods:
        in_specs += [pl.BlockSpec((1, d), lambda i, j: (0, 0)), vec(0), vec(1)]
        args += [g, mod3, mod3]
    in_specs.append(pl.BlockSpec((tn, d), lambda i, j: (j, 0)) if trans_w
                    else pl.BlockSpec((d, tn), lambda i, j: (0, j)))
    args.append(w)
    out_specs = pl.BlockSpec((tm, tn), lambda i, j: (i, j))
    out_shape = jax.ShapeDtypeStruct((t, n), BF16)
    if wg is not None:
        in_specs.append(pl.BlockSpec((LANES, d), lambda i, j: (0, 0)))
        args.append(wg)
        out_specs = [out_specs, pl.BlockSpec((tm, LANES), lambda i, j: (i, 0))]
        out_shape = [out_shape, jax.ShapeDtypeStruct((t, LANES), F32)]
    return pl.pallas_call(
        functools.partial(_proj_kernel, n_mod=n_mod, trans_w=trans_w, with_gates=wg is not None,
                          n1_tiles=n1 // tn, lead_tiles=lead_cols // tn, lead_scale=lead_scale),
        grid=(nt, nj),
        in_specs=in_specs,
        out_specs=out_specs,
        out_shape=out_shape,
        scratch_shapes=[pltpu.VMEM((n_mod, tm, d), BF16)],
        compiler_params=pltpu.CompilerParams(
            dimension_semantics=("arbitrary", "arbitrary"), vmem_limit_bytes=VMEM_LIMIT),
        name=name,
    )(*args)


def _mlstm_kernel(gb_ref, sh_ref, q_ref, k_ref, v_ref, o_ref, z_ref, gr_ref, cwq_ref, cwk_ref,
                  cbq_ref, cbk_ref, gh_ref, y_ref, c_scr, n_scr, m_scr, hq_scr, hk_scr):
    group = pl.program_id(1)
    L = q_ref.shape[0]
    hp = q_ref.shape[1] // A_DK

    @pl.when(pl.program_id(2) == 0)
    def _():
        c_scr[...] = jnp.zeros_like(c_scr)
        n_scr[...] = jnp.zeros_like(n_scr)
        m_scr[...] = jnp.zeros_like(m_scr)
        hq_scr[...] = jnp.zeros_like(hq_scr)
        hk_scr[...] = jnp.zeros_like(hk_scr)

    row8 = lax.broadcasted_iota(jnp.int32, (SUBLANES, LANES), 0)

    def conv_silu(u_ref, halo_scr, w_ref, b_ref):
        blocks = []
        for cb in range(u_ref.shape[1] // LANES):
            cols = slice(cb * LANES, (cb + 1) * LANES)
            ub = u_ref[:, cols]
            u = ub.astype(F32)
            shifted = jnp.dot(sh_ref[...], ub, preferred_element_type=F32)
            halo = halo_scr[:, cols]
            acc = b_ref[:, cols]
            for j in range(CONV_W - 1):
                s = CONV_W - 1 - j
                blk = shifted[(s - 1) * L:s * L]
                fix = jnp.where(row8 < s, pltpu.roll(halo, s, 0), 0.0)
                blk = jnp.concatenate([blk[:SUBLANES] + fix, blk[SUBLANES:]], axis=0)
                acc = acc + blk * w_ref[j:j + 1, cols]
            acc = acc + u * w_ref[CONV_W - 1:CONV_W, cols]
            halo_scr[:, cols] = u[L - SUBLANES:]
            blocks.append(_silu(acc))
        bph = A_DK // LANES
        return [jnp.concatenate(blocks[h * bph:(h + 1) * bph], axis=1) for h in range(len(blocks) // bph)]

    q_heads = conv_silu(q_ref, hq_scr, cwq_ref, cbq_ref)
    k_heads = [kh * (A_DK ** -0.5) for kh in conv_silu(k_ref, hk_scr, cwk_ref, cbk_ref)]

    row = lax.broadcasted_iota(jnp.int32, (L, L), 0)
    col = lax.broadcasted_iota(jnp.int32, (L, L), 1)
    causal = col <= row
    eye = col == row

    for h in range(hp):
        head = group * hp + h
        q = q_heads[h]
        k = k_heads[h]
        vcols = slice(h * A_DV, (h + 1) * A_DV)
        vb = v_ref[:, vcols]

        li_row = gr_ref[0, h] + gb_ref[head]
        lf_row = _log_sigmoid(gr_ref[1, h] + gb_ref[A_HEADS + head])

        b_col = jnp.sum(jnp.where(causal, lf_row, 0.0), axis=1, keepdims=True)
        b_row = jnp.sum(jnp.where(eye, b_col, 0.0), axis=0, keepdims=True)
        a_row = li_row - b_row
        dm = jnp.where(causal, b_col + a_row, NEG)
        m_prev = m_scr[h]
        inter = b_col + m_prev
        m_t = jnp.maximum(inter, jnp.max(dm, axis=1, keepdims=True))
        w_intra = jnp.exp(dm - m_t)
        w_inter = jnp.exp(inter - m_t)

        qb = q.astype(BF16)
        kb = k.astype(BF16)
        s = lax.dot_general(qb, kb, (((1,), (1,)), ((), ())), preferred_element_type=F32) * w_intra
        c_old = c_scr[h]
        n_old = n_scr[h]
        num = (w_inter * jnp.dot(qb, c_old.astype(BF16), preferred_element_type=F32)
               + jnp.dot(s.astype(BF16), vb, preferred_element_type=F32))
        den = (w_inter * jnp.sum(q * n_old, axis=1, keepdims=True)
               + jnp.sum(s, axis=1, keepdims=True))
        r = 1.0 / jnp.maximum(jnp.abs(den), jnp.exp(-m_t))
        rn = r * lax.rsqrt(r * r * jnp.mean(num * num, axis=-1, keepdims=True) + EPS)
        hn = num * rn * gh_ref[:, vcols]
        gate = _sigmoid(o_ref[:, vcols]) * _silu(z_ref[:, vcols])
        y = hn * gate.astype(F32)
        y_ref[:, vcols] = y.astype(y_ref.dtype)

        b_last = b_col[L - 1:L, :]
        g_row = b_last + a_row
        m_new = jnp.maximum(b_last + m_prev, jnp.max(g_row, axis=1, keepdims=True))
        decay = jnp.exp(b_last + m_prev - m_new)
        ws_row = jnp.exp(g_row - m_new)
        ws_col = jnp.sum(jnp.where(eye, ws_row, 0.0), axis=1, keepdims=True)
        kw = k * ws_col
        c_scr[h] = decay * c_old + lax.dot_general(
            kw.astype(BF16), vb, (((0,), (0,)), ((), ())), preferred_element_type=F32)
        n_scr[h] = decay * n_old + jnp.sum(kw, axis=0, keepdims=True)
        m_scr[h] = m_new


def _shift_matrix(L):
    t = jnp.arange(L)
    blocks = [(t[:, None] - s == t[None, :]) for s in range(1, CONV_W)]
    return jnp.concatenate(blocks, axis=0).astype(BF16)


def _mlstm_call(proj, gates_row, gate_b, conv_w, conv_b, g_head):
    bn, seq, _ = proj.shape
    L = MLSTM_CHUNK
    hp = MLSTM_HPG
    ng = A_HEADS // hp
    qw, vw = hp * A_DK, hp * A_DV
    v0 = 2 * A_QK // vw
    return pl.pallas_call(
        _mlstm_kernel,
        grid=(bn, ng, seq // L),
        in_specs=[
            pl.BlockSpec(memory_space=pltpu.SMEM),
            pl.BlockSpec(((CONV_W - 1) * L, L), lambda b, g, i: (0, 0)),
            pl.BlockSpec((None, L, qw), lambda b, g, i: (b, i, g)),
            pl.BlockSpec((None, L, qw), lambda b, g, i: (b, i, ng + g)),
            pl.BlockSpec((None, L, vw), lambda b, g, i: (b, i, v0 + g)),
            pl.BlockSpec((None, L, vw), lambda b, g, i: (b, i, v0 + ng + g)),
            pl.BlockSpec((None, L, vw), lambda b, g, i: (b, i, v0 + 2 * ng + g)),
            pl.BlockSpec((None, 2, hp, 1, L), lambda b, g, i: (b, 0, g, 0, i)),
            pl.BlockSpec((CONV_W, qw), lambda b, g, i: (0, g)),
            pl.BlockSpec((CONV_W, qw), lambda b, g, i: (0, ng + g)),
            pl.BlockSpec((1, qw), lambda b, g, i: (0, g)),
            pl.BlockSpec((1, qw), lambda b, g, i: (0, ng + g)),
            pl.BlockSpec((1, vw), lambda b, g, i: (0, g)),
        ],
        out_specs=pl.BlockSpec((None, L, vw), lambda b, g, i: (b, i, g)),
        out_shape=jax.ShapeDtypeStruct((bn, seq, A_INNER), BF16),
        scratch_shapes=[
            pltpu.VMEM((hp, A_DK, A_DV), F32),
            pltpu.VMEM((hp, 1, A_DK), F32),
            pltpu.VMEM((hp, 1, 1), F32),
            pltpu.VMEM((SUBLANES, qw), F32),
            pltpu.VMEM((SUBLANES, qw), F32),
        ],
        compiler_params=pltpu.CompilerParams(
            dimension_semantics=("arbitrary", "arbitrary", "arbitrary"), vmem_limit_bytes=VMEM_LIMIT),
        name="mlstm_core",
    )(gate_b, _shift_matrix(L), proj, proj, proj, proj, proj, gates_row, conv_w, conv_w, conv_b, conv_b, g_head)


def _out_kernel(y_ref, w_ref, x_ref, g_ref, gate_ref, o_ref):
    y = jnp.dot(y_ref[...], w_ref[...], preferred_element_type=F32)
    yn = y * lax.rsqrt(jnp.mean(y * y, axis=-1, keepdims=True) + EPS) * g_ref[...]
    o_ref[...] = x_ref[...] + gate_ref[...] * yn


def _out_call(y2, w, x2, g, mod3, gate_blk, seq, name):
    t, kdim = y2.shape
    d = w.shape[1]
    tm = 512
    per_batch = seq // tm
    return pl.pallas_call(
        _out_kernel,
        grid=(t // tm,),
        in_specs=[
            pl.BlockSpec((tm, kdim), lambda i: (i, 0)),
            pl.BlockSpec((kdim, d), lambda i: (0, 0), pipeline_mode=pl.Buffered(1)),
            pl.BlockSpec((tm, d), lambda i: (i, 0)),
            pl.BlockSpec((1, d), lambda i: (0, 0)),
            pl.BlockSpec((None, 1, d), lambda i: (i // per_batch, 0, gate_blk)),
        ],
        out_specs=pl.BlockSpec((tm, d), lambda i: (i, 0)),
        out_shape=jax.ShapeDtypeStruct((t, d), F32),
        compiler_params=pltpu.CompilerParams(
            dimension_semantics=("arbitrary",), vmem_limit_bytes=VMEM_LIMIT),
        name=name,
    )(y2, w, x2, g, mod3)


def _bias_kernel(tbl_ref, o_ref, base_scr):
    head = pl.program_id(0)
    nb, lq, _ = o_ref.shape
    width = nb * LANES + lq

    @pl.when(head == 0)
    def _():
        u = lax.broadcasted_iota(jnp.int32, (tbl_ref.shape[1], width), 1)
        u = jnp.where(u >= nb * LANES, u - width, u)
        idx = jnp.clip(PAD - u, -MAX_REL, MAX_REL) + MAX_REL
        c = lax.broadcasted_iota(jnp.int32, (tbl_ref.shape[1], width), 0)
        onehot = jnp.where(c == idx, 1.0, 0.0)
        base_scr[...] = jnp.dot(tbl_ref[...], onehot, preferred_element_type=F32,
                                precision=lax.Precision.HIGHEST)

    base = base_scr[pl.ds(head, 1), :]
    tile = jnp.broadcast_to(base, (lq, width))
    r = lax.broadcasted_iota(jnp.int32, (lq, width), 0)
    tile = pltpu.roll(tile, 0, 1, stride=1, stride_axis=0)
    uu = lax.broadcasted_iota(jnp.int32, (lq, width), 1)
    cq = r // CHUNK
    ck = uu // CHUNK - LEFT_CHUNKS
    visible = (ck <= cq) & (ck >= cq - LEFT_CHUNKS)
    tile = jnp.where(visible, tile * LOG2E, NEG)
    for j in range(nb):
        o_ref[j] = tile[:, j * LANES:(j + 1) * LANES]


def _bias_call(rel_table):
    nrel = rel_table.shape[1]
    ncols = -(-nrel // LANES) * LANES
    tbl = jnp.pad(rel_table, ((0, 0), (0, ncols - nrel)))
    return pl.pallas_call(
        _bias_kernel,
        grid=(B_HEADS,),
        in_specs=[pl.BlockSpec((B_HEADS, ncols), lambda h: (0, 0))],
        out_specs=pl.BlockSpec((None, ATTN_NB, ATTN_LQ, LANES), lambda h: (h, 0, 0, 0)),
        out_shape=jax.ShapeDtypeStruct((B_HEADS, ATTN_NB, ATTN_LQ, LANES), F32),
        scratch_shapes=[pltpu.VMEM((B_HEADS, ATTN_NB * LANES + ATTN_LQ), F32)],
        compiler_params=pltpu.CompilerParams(
            dimension_semantics=("arbitrary",), vmem_limit_bytes=VMEM_LIMIT),
        name="relpos_bias",
    )(tbl)


def _attn_kernel(q_ref, z_ref, k_ref, v_ref, wb_ref, y_ref):
    lq = q_ref.shape[0]
    wk = PAD + lq
    start = pl.program_id(2) * lq
    wstart = pl.multiple_of(jnp.maximum(start - PAD, 0), LANES)
    off = jnp.maximum(PAD - start, 0) // LANES
    for h in range(q_ref.shape[1] // B_DH):
        cols = slice(h * B_DH, (h + 1) * B_DH)
        kwin = k_ref[pl.ds(wstart, wk), cols]
        vwin = v_ref[pl.ds(wstart, wk), cols]
        s = lax.dot_general(q_ref[:, cols], kwin, (((1,), (1,)), ((), ())), preferred_element_type=F32)
        s = s + jnp.concatenate([wb_ref[h, off + j] for j in range(wk // LANES)], axis=1)
        m = jnp.max(s, axis=1, keepdims=True)
        p = jnp.exp2(s - m)
        inv = 1.0 / jnp.sum(p, axis=1, keepdims=True)
        o = jnp.dot(p.astype(BF16), vwin, preferred_element_type=F32) * inv
        y_ref[:, cols] = (o * _silu(z_ref[:, cols].astype(F32))).astype(y_ref.dtype)


def _attn_call(qzkv, wb):
    bn, seq, _ = qzkv.shape
    lq = ATTN_LQ
    gw = ATTN_HPG * B_DH
    ng = B_HEADS // ATTN_HPG
    resident = pl.Buffered(1)
    return pl.pallas_call(
        _attn_kernel,
        grid=(ng, bn, seq // lq),
        in_specs=[
            pl.BlockSpec((None, lq, gw), lambda g, b, i: (b, i, g)),
            pl.BlockSpec((None, lq, gw), lambda g, b, i: (b, i, ng + g)),
            pl.BlockSpec((None, seq, gw), lambda g, b, i: (b, 0, 2 * ng + g)),
            pl.BlockSpec((None, seq, gw), lambda g, b, i: (b, 0, 3 * ng + g)),
            pl.BlockSpec((ATTN_HPG, ATTN_NB, lq, LANES), lambda g, b, i: (g, 0, 0, 0), pipeline_mode=resident),
        ],
        out_specs=pl.BlockSpec((None, lq, gw), lambda g, b, i: (b, i, g)),
        out_shape=jax.ShapeDtypeStruct((bn, seq, B_WIDTH), BF16),
        compiler_params=pltpu.CompilerParams(
            dimension_semantics=("arbitrary", "arbitrary", "arbitrary"), vmem_limit_bytes=VMEM_LIMIT),
        name="band_attention",
    )(qzkv, qzkv, qzkv, qzkv, wb)


def kernel(x, c, ada_w, ada_b, g_pre, g_post, a_w_in, a_conv_w, a_conv_b, a_gate_b, a_g_head, a_w_out,
           kv_ada_w, kv_ada_b, kv_g, kv_w, b_w_in, b_rel, b_w_out):
    bn, seq, d = x.shape
    t = bn * seq
    assert bn <= SUBLANES and seq % 1024 == 0 and seq >= PAD + ATTN_LQ
    x2 = x.reshape(t, d)
    c8 = jnp.pad(c.astype(F32), ((0, SUBLANES - bn), (0, 0)))

    mod0 = _mod_call(c8, ada_w, ada_b, layer=0).reshape(SUBLANES, 1, 3 * d)
    n_main = 2 * A_QK + 3 * A_INNER
    w_main_t, w_gate_t = _cast_w_in_call(a_w_in[0].T, n_main)
    proj, gates = _proj_call(x2, [(g_pre[0].reshape(1, d), mod0)], w_main_t, w_gate_t, seq, "mlstm_in_proj",
                             tn=2048, trans_w=True)
    gates_row = gates[:, :2 * A_HEADS].reshape(bn, seq, 2, A_HEADS).transpose(0, 2, 3, 1)
    gates_row = gates_row.reshape(bn, 2, A_HEADS, 1, seq)
    y0 = _mlstm_call(proj.reshape(bn, seq, n_main), gates_row, a_gate_b[0], a_conv_w[0],
                     a_conv_b[0].reshape(1, 2 * A_QK), a_g_head[0].reshape(1, A_INNER))
    x2 = _out_call(y0.reshape(t, A_INNER), a_w_out[0].astype(BF16), x2, g_post[0].reshape(1, d),
                   mod0, 2, seq, "mlstm_out_proj")

    mod1 = _mod_call(c8, ada_w, ada_b, layer=1).reshape(SUBLANES, 1, 3 * d)
    kmod = _mod_call(c8, kv_ada_w, kv_ada_b).reshape(SUBLANES, 1, 2 * d)
    w_qzkv = _cast_pair_call(b_w_in[0], kv_w)
    qzkv = _proj_call(x2, [(g_pre[1].reshape(1, d), mod1), (kv_g.reshape(1, d), kmod)], w_qzkv, None, seq,
                      "attn_qzkv_proj", tn=1024, n1=2 * B_WIDTH,
                      lead_cols=B_WIDTH, lead_scale=(B_DH ** -0.5) * LOG2E)
    wb = _bias_call(b_rel[0])
    y1 = _attn_call(qzkv.reshape(bn, seq, 4 * B_WIDTH), wb)
    x2 = _out_call(y1.reshape(t, B_WIDTH), b_w_out[0].astype(BF16), x2, g_post[1].reshape(1, d),
                   mod1, 2, seq, "attn_out_proj")
    return x2.reshape(bn, seq, d)
```

```python
import functools

import jax
import jax.numpy as jnp
from jax import lax
from jax.experimental import pallas as pl
from jax.experimental.pallas import tpu as pltpu

F32 = jnp.float32
BF16 = jnp.bfloat16

EPS = 1e-6
CHUNK = 64
A_HEADS = 8
A_DK = 256
A_DV = 512
A_QK = A_HEADS * A_DK
A_INNER = A_HEADS * A_DV
CONV_W = 4
B_HEADS = 16
B_DH = 128
B_WIDTH = B_HEADS * B_DH
LEFT_CHUNKS = 8
PAD = LEFT_CHUNKS * CHUNK
MAX_REL = 128
NEG = -1e30

LANES = 128
SUBLANES = 8

LOG2E = 1.4426950408889634

MLSTM_CHUNK = 256
MLSTM_HPG = 8
ATTN_LQ = 256
ATTN_HPG = 8
ATTN_WK = PAD + ATTN_LQ
ATTN_NB = (PAD + ATTN_WK) // LANES

VMEM_LIMIT = 56 * 1024 * 1024


def _sigmoid(v):
    return 0.5 * jnp.tanh(0.5 * v) + 0.5


def _silu(v):
    hv = 0.5 * v
    return hv * jnp.tanh(hv) + hv


def _log_sigmoid(v):
    return -(jnp.maximum(-v, 0.0) + jnp.log1p(jnp.exp(-jnp.abs(v))))


def _mod_kernel(c_ref, w_ref, b_ref, o_ref):
    sc = _silu(c_ref[...])
    o_ref[...] = jnp.dot(sc.astype(BF16), w_ref[...].astype(BF16),
                         preferred_element_type=F32) + b_ref[...]


def _mod_call(c8, w, b, layer=None):
    d = c8.shape[1]
    n = w.shape[-1]
    tn = 1024
    if layer is None:
        w_spec = pl.BlockSpec((d, tn), lambda j: (0, j))
        b_spec = pl.BlockSpec((1, tn), lambda j: (0, j))
        b2 = b.reshape(1, n)
    else:
        w_spec = pl.BlockSpec((None, d, tn), lambda j: (layer, 0, j))
        b_spec = pl.BlockSpec((None, 1, tn), lambda j: (layer, 0, j))
        b2 = b.reshape(b.shape[0], 1, n)
    return pl.pallas_call(
        _mod_kernel,
        grid=(n // tn,),
        in_specs=[pl.BlockSpec((SUBLANES, d), lambda j: (0, 0)), w_spec, b_spec],
        out_specs=pl.BlockSpec((SUBLANES, tn), lambda j: (0, j)),
        out_shape=jax.ShapeDtypeStruct((SUBLANES, n), F32),
        compiler_params=pltpu.CompilerParams(
            dimension_semantics=("arbitrary",), vmem_limit_bytes=VMEM_LIMIT),
        name="adaln_mod",
    )(c8, w, b2)


def _cast_kernel(w_ref, wg_ref, o_ref, og_ref):
    o_ref[...] = w_ref[...].astype(BF16)

    @pl.when(pl.program_id(0) == 0)
    def _():
        r = lax.broadcasted_iota(jnp.int32, wg_ref.shape, 0)
        og_ref[...] = jnp.where(r < 2 * A_HEADS, wg_ref[...], 0.0).astype(BF16)


def _cast_w_in_call(w_t, n_main):
    _, d = w_t.shape
    tn = 1024
    return pl.pallas_call(
        _cast_kernel,
        grid=(n_main // tn,),
        in_specs=[
            pl.BlockSpec((tn, d), lambda j: (j, 0)),
            pl.BlockSpec((LANES, d), lambda j: (n_main // LANES, 0)),
        ],
        out_specs=[
            pl.BlockSpec((tn, d), lambda j: (j, 0)),
            pl.BlockSpec((LANES, d), lambda j: (0, 0)),
        ],
        out_shape=[jax.ShapeDtypeStruct((n_main, d), BF16), jax.ShapeDtypeStruct((LANES, d), BF16)],
        compiler_params=pltpu.CompilerParams(
            dimension_semantics=("arbitrary",), vmem_limit_bytes=VMEM_LIMIT),
        name="cast_w_in",
    )(w_t, w_t)


def _cast_pair_kernel(w1_ref, w2_ref, o_ref, *, n1_tiles):
    @pl.when(pl.program_id(0) < n1_tiles)
    def _():
        o_ref[...] = w1_ref[...].astype(BF16)

    @pl.when(pl.program_id(0) >= n1_tiles)
    def _():
        o_ref[...] = w2_ref[...].astype(BF16)


def _cast_pair_call(w1, w2):
    d, n1 = w1.shape
    n2 = w2.shape[1]
    tn = 512
    n1_tiles = n1 // tn
    return pl.pallas_call(
        functools.partial(_cast_pair_kernel, n1_tiles=n1_tiles),
        grid=((n1 + n2) // tn,),
        in_specs=[
            pl.BlockSpec((d, tn), lambda j: (0, jnp.minimum(j, n1_tiles - 1))),
            pl.BlockSpec((d, tn), lambda j: (0, jnp.maximum(j - n1_tiles, 0))),
        ],
        out_specs=pl.BlockSpec((d, tn), lambda j: (0, j)),
        out_shape=jax.ShapeDtypeStruct((d, n1 + n2), BF16),
        compiler_params=pltpu.CompilerParams(
            dimension_semantics=("arbitrary",), vmem_limit_bytes=VMEM_LIMIT),
        name="cast_w_qzkv",
    )(w1, w2)


_NT =(((1,), (1,)), ((), ()))


def _modulated(xn, g_ref, shift_ref, scale_ref):
    return ((xn * g_ref[...]) * (1.0 + scale_ref[...]) + shift_ref[...]).astype(BF16)


def _proj_kernel(*refs, n_mod, trans_w, with_gates, n1_tiles, lead_tiles, lead_scale):
    x_ref = refs[0]
    mod_refs = refs[1:1 + 3 * n_mod]
    rest = list(refs[1 + 3 * n_mod:])
    w_ref = rest.pop(0)
    wg_ref = rest.pop(0) if with_gates else None
    o_ref = rest.pop(0)
    og_ref = rest.pop(0) if with_gates else None
    h_scr = rest.pop(0)
    j = pl.program_id(1)

    def project(h):
        if trans_w:
            acc = lax.dot_general(h, w_ref[...], _NT, preferred_element_type=F32)
        else:
            acc = jnp.dot(h, w_ref[...], preferred_element_type=F32)
        if lead_tiles:
            acc = acc * jnp.where(j < lead_tiles, lead_scale, 1.0)
        o_ref[...] = acc.astype(o_ref.dtype)

    @pl.when(j == 0)
    def _():
        x = x_ref[...]
        xn = x * lax.rsqrt(jnp.mean(x * x, axis=-1, keepdims=True) + EPS)
        hs = []
        for k in range(n_mod):
            hb = _modulated(xn, *mod_refs[3 * k:3 * k + 3])
            h_scr[k] = hb
            hs.append(hb)
        if with_gates:
            og_ref[...] = lax.dot_general(hs[0], wg_ref[...], _NT, preferred_element_type=F32)
        project(hs[0])

    @pl.when(j > 0)
    def _():
        project(h_scr[jnp.where(j >= n1_tiles, 1, 0) if n_mod == 2 else 0])


def _proj_call(x2, mods, w, wg, seq, name, *, tn, trans_w=False, n1=None, lead_cols=0, lead_scale=1.0):
    t, d = x2.shape
    n = w.shape[0] if trans_w else w.shape[1]
    tm = 1024
    nt, nj = t // tm, n // tn
    per_batch = seq // tm
    n_mod = len(mods)
    n1 = n if n1 is None else n1
    assert lead_cols % tn == 0 and n1 % tn == 0 and n % tn == 0 and n1 >= tn

    vec = lambda blk: pl.BlockSpec((None, 1, d), lambda i, j: (i // per_batch, 0, blk))
    in_specs = [pl.BlockSpec((tm, d), lambda i, j: (i, 0))]
    args = [x2]
    for g, mod3 in mods:
        in_specs += [pl.BlockSpec((1, d), lambda i, j: (0, 0)), vec(0), vec(1)]
        args += [g, mod3, mod3]
    in_specs.append(pl.BlockSpec((tn, d), lambda i, j: (j, 0)) if trans_w
                    else pl.BlockSpec((d, tn), lambda i, j: (0, j)))
    args.append(w)
    out_specs = pl.BlockSpec((tm, tn), lambda i, j: (i, j))
    out_shape = jax.ShapeDtypeStruct((t, n), BF16)
    if wg is not None:
        in_specs.append(pl.BlockSpec((LANES, d), lambda i, j: (0, 0)))
        args.append(wg)
        out_specs = [out_specs, pl.BlockSpec((tm, LANES), lambda i, j: (i, 0))]
        out_shape = [out_shape, jax.ShapeDtypeStruct((t, LANES), F32)]
    return pl.pallas_call(
        functools.partial(_proj_kernel, n_mod=n_mod, trans_w=trans_w, with_gates=wg is not None,
                          n1_tiles=n1 // tn, lead_tiles=lead_cols // tn, lead_scale=lead_scale),
        grid=(nt, nj),
        in_specs=in_specs,
        out_specs=out_specs,
        out_shape=out_shape,
        scratch_shapes=[pltpu.VMEM((n_mod, tm, d), BF16)],
        compiler_params=pltpu.CompilerParams(
            dimension_semantics=("arbitrary", "arbitrary"), vmem_limit_bytes=VMEM_LIMIT),
        name=name,
    )(*args)


def _mlstm_kernel(gb_ref, sh_ref, q_ref, k_ref, v_ref, o_ref, z_ref, gr_ref, cwq_ref, cwk_ref,
                  cbq_ref, cbk_ref, gh_ref, wout_ref, x_ref, gpost_ref, gate_ref, xo_ref,
                  c_scr, n_scr, m_scr, hq_scr, hk_scr):
    group = pl.program_id(1)
    L = q_ref.shape[0]
    hp = q_ref.shape[1] // A_DK

    @pl.when(pl.program_id(2) == 0)
    def _():
        c_scr[...] = jnp.zeros_like(c_scr)
        n_scr[...] = jnp.zeros_like(n_scr)
        m_scr[...] = jnp.zeros_like(m_scr)
        hq_scr[...] = jnp.zeros_like(hq_scr)
        hk_scr[...] = jnp.zeros_like(hk_scr)

    row8 = lax.broadcasted_iota(jnp.int32, (SUBLANES, LANES), 0)

    def conv_silu(u_ref, halo_scr, w_ref, b_ref):
        blocks = []
        for cb in range(u_ref.shape[1] // LANES):
            cols = slice(cb * LANES, (cb + 1) * LANES)
            ub = u_ref[:, cols]
            u = ub.astype(F32)
            shifted = jnp.dot(sh_ref[...], ub, preferred_element_type=F32)
            halo = halo_scr[:, cols]
            acc = b_ref[:, cols]
            for j in range(CONV_W - 1):
                s = CONV_W - 1 - j
                blk = shifted[(s - 1) * L:s * L]
                fix = jnp.where(row8 < s, pltpu.roll(halo, s, 0), 0.0)
                blk = jnp.concatenate([blk[:SUBLANES] + fix, blk[SUBLANES:]], axis=0)
                acc = acc + blk * w_ref[j:j + 1, cols]
            acc = acc + u * w_ref[CONV_W - 1:CONV_W, cols]
            halo_scr[:, cols] = u[L - SUBLANES:]
            blocks.append(_silu(acc))
        bph = A_DK // LANES
        return [jnp.concatenate(blocks[h * bph:(h + 1) * bph], axis=1) for h in range(len(blocks) // bph)]

    q_heads = conv_silu(q_ref, hq_scr, cwq_ref, cbq_ref)
    k_heads = [kh * (A_DK ** -0.5) for kh in conv_silu(k_ref, hk_scr, cwk_ref, cbk_ref)]

    row = lax.broadcasted_iota(jnp.int32, (L, L), 0)
    col = lax.broadcasted_iota(jnp.int32, (L, L), 1)
    causal = col <= row
    eye = col == row

    for h in range(hp):
        head = group * hp + h
        q = q_heads[h]
        k = k_heads[h]
        vcols = slice(h * A_DV, (h + 1) * A_DV)
        vb = v_ref[:, vcols]

        li_row = gr_ref[0, h] + gb_ref[head]
        lf_row = _log_sigmoid(gr_ref[1, h] + gb_ref[A_HEADS + head])

        b_col = jnp.sum(jnp.where(causal, lf_row, 0.0), axis=1, keepdims=True)
        b_row = jnp.sum(jnp.where(eye, b_col, 0.0), axis=0, keepdims=True)
        a_row = li_row - b_row
        dm = jnp.where(causal, b_col + a_row, NEG)
        m_prev = m_scr[h]
        inter = b_col + m_prev
        m_t = jnp.maximum(inter, jnp.max(dm, axis=1, keepdims=True))
        w_intra = jnp.exp(dm - m_t)
        w_inter = jnp.exp(inter - m_t)

        qb = q.astype(BF16)
        kb = k.astype(BF16)
        s = lax.dot_general(qb, kb, (((1,), (1,)), ((), ())), preferred_element_type=F32) * w_intra
        c_old = c_scr[h]
        n_old = n_scr[h]
        num = (w_inter * jnp.dot(qb, c_old.astype(BF16), preferred_element_type=F32)
               + jnp.dot(s.astype(BF16), vb, preferred_element_type=F32))
        den = (w_inter * jnp.sum(q * n_old, axis=1, keepdims=True)
               + jnp.sum(s, axis=1, keepdims=True))
        r = 1.0 / jnp.maximum(jnp.abs(den), jnp.exp(-m_t))
        rn = r * lax.rsqrt(r * r * jnp.mean(num * num, axis=-1, keepdims=True) + EPS)
        hn = num * rn * gh_ref[:, vcols]
        gate = _sigmoid(o_ref[:, vcols]) * _silu(z_ref[:, vcols])
        y = hn * gate.astype(F32)
        part = jnp.dot(y.astype(BF16), wout_ref[vcols, :], preferred_element_type=F32)
        mixed = part if h == 0 else mixed + part

        b_last = b_col[L - 1:L, :]
        g_row = b_last + a_row
        m_new = jnp.maximum(b_last + m_prev, jnp.max(g_row, axis=1, keepdims=True))
        decay = jnp.exp(b_last + m_prev - m_new)
        ws_row = jnp.exp(g_row - m_new)
        ws_col = jnp.sum(jnp.where(eye, ws_row, 0.0), axis=1, keepdims=True)
        kw = k * ws_col
        c_scr[h] = decay * c_old + lax.dot_general(
            kw.astype(BF16), vb, (((0,), (0,)), ((), ())), preferred_element_type=F32)
        n_scr[h] = decay * n_old + jnp.sum(kw, axis=0, keepdims=True)
        m_scr[h] = m_new

    yn = mixed * lax.rsqrt(jnp.mean(mixed * mixed, axis=-1, keepdims=True) + EPS) * gpost_ref[...]
    xo_ref[...] = x_ref[...] + gate_ref[...] * yn


def _shift_matrix(L):
    t = jnp.arange(L)
    blocks = [(t[:, None] - s == t[None, :]) for s in range(1, CONV_W)]
    return jnp.concatenate(blocks, axis=0).astype(BF16)


def _mlstm_call(proj, gates_row, gate_b, conv_w, conv_b, g_head, w_out, x3, g_post, mod3):
    bn, seq, _ = proj.shape
    d = x3.shape[2]
    L = MLSTM_CHUNK
    hp = MLSTM_HPG
    assert hp == A_HEADS
    ng = A_HEADS // hp
    qw, vw = hp * A_DK, hp * A_DV
    v0 = 2 * A_QK // vw
    return pl.pallas_call(
        _mlstm_kernel,
        grid=(bn, ng, seq // L),
        in_specs=[
            pl.BlockSpec(memory_space=pltpu.SMEM),
            pl.BlockSpec(((CONV_W - 1) * L, L), lambda b, g, i: (0, 0)),
            pl.BlockSpec((None, L, qw), lambda b, g, i: (b, i, g)),
            pl.BlockSpec((None, L, qw), lambda b, g, i: (b, i, ng + g)),
            pl.BlockSpec((None, L, vw), lambda b, g, i: (b, i, v0 + g)),
            pl.BlockSpec((None, L, vw), lambda b, g, i: (b, i, v0 + ng + g)),
            pl.BlockSpec((None, L, vw), lambda b, g, i: (b, i, v0 + 2 * ng + g)),
            pl.BlockSpec((None, 2, hp, 1, L), lambda b, g, i: (b, 0, g, 0, i)),
            pl.BlockSpec((CONV_W, qw), lambda b, g, i: (0, g)),
            pl.BlockSpec((CONV_W, qw), lambda b, g, i: (0, ng + g)),
            pl.BlockSpec((1, qw), lambda b, g, i: (0, g)),
            pl.BlockSpec((1, qw), lambda b, g, i: (0, ng + g)),
            pl.BlockSpec((1, vw), lambda b, g, i: (0, g)),
            pl.BlockSpec((A_INNER, d), lambda b, g, i: (0, 0), pipeline_mode=pl.Buffered(1)),
            pl.BlockSpec((None, L, d), lambda b, g, i: (b, i, 0)),
            pl.BlockSpec((1, d), lambda b, g, i: (0, 0)),
            pl.BlockSpec((None, 1, d), lambda b, g, i: (b, 0, 2)),
        ],
        out_specs=pl.BlockSpec((None, L, d), lambda b, g, i: (b, i, 0)),
        out_shape=jax.ShapeDtypeStruct((bn, seq, d), F32),
        scratch_shapes=[
            pltpu.VMEM((hp, A_DK, A_DV), F32),
            pltpu.VMEM((hp, 1, A_DK), F32),
            pltpu.VMEM((hp, 1, 1), F32),
            pltpu.VMEM((SUBLANES, qw), F32),
            pltpu.VMEM((SUBLANES, qw), F32),
        ],
        compiler_params=pltpu.CompilerParams(
            dimension_semantics=("arbitrary", "arbitrary", "arbitrary"), vmem_limit_bytes=VMEM_LIMIT),
        name="mlstm_core",
    )(gate_b, _shift_matrix(L), proj, proj, proj, proj, proj, gates_row, conv_w, conv_w, conv_b, conv_b, g_head,
      w_out, x3, g_post, mod3)


def _out_kernel(y_ref, w_ref, x_ref, g_ref, gate_ref, o_ref):
    y = jnp.dot(y_ref[...], w_ref[...], preferred_element_type=F32)
    yn = y * lax.rsqrt(jnp.mean(y * y, axis=-1, keepdims=True) + EPS) * g_ref[...]
    o_ref[...] = x_ref[...] + gate_ref[...] * yn


def _out_call(y2, w, x2, g, mod3, gate_blk, seq, name):
    t, kdim = y2.shape
    d = w.shape[1]
    tm = 512
    per_batch = seq // tm
    return pl.pallas_call(
        _out_kernel,
        grid=(t // tm,),
        in_specs=[
            pl.BlockSpec((tm, kdim), lambda i: (i, 0)),
            pl.BlockSpec((kdim, d), lambda i: (0, 0), pipeline_mode=pl.Buffered(1)),
            pl.BlockSpec((tm, d), lambda i: (i, 0)),
            pl.BlockSpec((1, d), lambda i: (0, 0)),
            pl.BlockSpec((None, 1, d), lambda i: (i // per_batch, 0, gate_blk)),
        ],
        out_specs=pl.BlockSpec((tm, d), lambda i: (i, 0)),
        out_shape=jax.ShapeDtypeStruct((t, d), F32),
        compiler_params=pltpu.CompilerParams(
            dimension_semantics=("arbitrary",), vmem_limit_bytes=VMEM_LIMIT),
        name=name,
    )(y2, w, x2, g, mod3)


def _bias_kernel(tbl_ref, o_ref, base_scr):
    head = pl.program_id(0)
    nb, lq, _ = o_ref.shape
    width = nb * LANES + lq

    @pl.when(head == 0)
    def _():
        u = lax.broadcasted_iota(jnp.int32, (tbl_ref.shape[1], width), 1)
        u = jnp.where(u >= nb * LANES, u - width, u)
        idx = jnp.clip(PAD - u, -MAX_REL, MAX_REL) + MAX_REL
        c = lax.broadcasted_iota(jnp.int32, (tbl_ref.shape[1], width), 0)
        onehot = jnp.where(c == idx, 1.0, 0.0)
        base_scr[...] = jnp.dot(tbl_ref[...], onehot, preferred_element_type=F32,
                                precision=lax.Precision.HIGHEST)

    base = base_scr[pl.ds(head, 1), :]
    tile = jnp.broadcast_to(base, (lq, width))
    r = lax.broadcasted_iota(jnp.int32, (lq, width), 0)
    tile = pltpu.roll(tile, 0, 1, stride=1, stride_axis=0)
    uu = lax.broadcasted_iota(jnp.int32, (lq, width), 1)
    cq = r // CHUNK
    ck = uu // CHUNK - LEFT_CHUNKS
    visible = (ck <= cq) & (ck >= cq - LEFT_CHUNKS)
    tile = jnp.where(visible, tile * LOG2E, NEG)
    for j in range(nb):
        o_ref[j] = tile[:, j * LANES:(j + 1) * LANES]


def _bias_call(rel_table):
    nrel = rel_table.shape[1]
    ncols = -(-nrel // LANES) * LANES
    tbl = jnp.pad(rel_table, ((0, 0), (0, ncols - nrel)))
    return pl.pallas_call(
        _bias_kernel,
        grid=(B_HEADS,),
        in_specs=[pl.BlockSpec((B_HEADS, ncols), lambda h: (0, 0))],
        out_specs=pl.BlockSpec((None, ATTN_NB, ATTN_LQ, LANES), lambda h: (h, 0, 0, 0)),
        out_shape=jax.ShapeDtypeStruct((B_HEADS, ATTN_NB, ATTN_LQ, LANES), F32),
        scratch_shapes=[pltpu.VMEM((B_HEADS, ATTN_NB * LANES + ATTN_LQ), F32)],
        compiler_params=pltpu.CompilerParams(
            dimension_semantics=("arbitrary",), vmem_limit_bytes=VMEM_LIMIT),
        name="relpos_bias",
    )(tbl)


def _attn_kernel(q_ref, z_ref, k_ref, v_ref, wb_ref, y_ref):
    lq = q_ref.shape[0]
    wk = PAD + lq
    start = pl.program_id(2) * lq
    wstart = pl.multiple_of(jnp.maximum(start - PAD, 0), LANES)
    off = jnp.maximum(PAD - start, 0) // LANES
    for h in range(q_ref.shape[1] // B_DH):
        cols = slice(h * B_DH, (h + 1) * B_DH)
        kwin = k_ref[pl.ds(wstart, wk), cols]
        vwin = v_ref[pl.ds(wstart, wk), cols]
        s = lax.dot_general(q_ref[:, cols], kwin, (((1,), (1,)), ((), ())), preferred_element_type=F32)
        s = s + jnp.concatenate([wb_ref[h, off + j] for j in range(wk // LANES)], axis=1)
        m = jnp.max(s, axis=1, keepdims=True)
        p = jnp.exp2(s - m)
        inv = 1.0 / jnp.sum(p, axis=1, keepdims=True)
        o = jnp.dot(p.astype(BF16), vwin, preferred_element_type=F32) * inv
        y_ref[:, cols] = (o * _silu(z_ref[:, cols].astype(F32))).astype(y_ref.dtype)


def _attn_call(qzkv, wb):
    bn, seq, _ = qzkv.shape
    lq = ATTN_LQ
    gw = ATTN_HPG * B_DH
    ng = B_HEADS // ATTN_HPG
    resident = pl.Buffered(1)
    return pl.pallas_call(
        _attn_kernel,
        grid=(ng, bn, seq // lq),
        in_specs=[
            pl.BlockSpec((None, lq, gw), lambda g, b, i: (b, i, g)),
            pl.BlockSpec((None, lq, gw), lambda g, b, i: (b, i, ng + g)),
            pl.BlockSpec((None, seq, gw), lambda g, b, i: (b, 0, 2 * ng + g)),
            pl.BlockSpec((None, seq, gw), lambda g, b, i: (b, 0, 3 * ng + g)),
            pl.BlockSpec((ATTN_HPG, ATTN_NB, lq, LANES), lambda g, b, i: (g, 0, 0, 0), pipeline_mode=resident),
        ],
        out_specs=pl.BlockSpec((None, lq, gw), lambda g, b, i: (b, i, g)),
        out_shape=jax.ShapeDtypeStruct((bn, seq, B_WIDTH), BF16),
        compiler_params=pltpu.CompilerParams(
            dimension_semantics=("arbitrary", "arbitrary", "arbitrary"), vmem_limit_bytes=VMEM_LIMIT),
        name="band_attention",
    )(qzkv, qzkv, qzkv, qzkv, wb)


def kernel(x, c, ada_w, ada_b, g_pre, g_post, a_w_in, a_conv_w, a_conv_b, a_gate_b, a_g_head, a_w_out,
           kv_ada_w, kv_ada_b, kv_g, kv_w, b_w_in, b_rel, b_w_out):
    bn, seq, d = x.shape
    t = bn * seq
    assert bn <= SUBLANES and seq % 1024 == 0 and seq >= PAD + ATTN_LQ
    x2 = x.reshape(t, d)
    c8 = jnp.pad(c.astype(F32), ((0, SUBLANES - bn), (0, 0)))

    mod0 = _mod_call(c8, ada_w, ada_b, layer=0).reshape(SUBLANES, 1, 3 * d)
    n_main = 2 * A_QK + 3 * A_INNER
    w_main_t, w_gate_t = _cast_w_in_call(a_w_in[0].T, n_main)
    proj, gates = _proj_call(x2, [(g_pre[0].reshape(1, d), mod0)], w_main_t, w_gate_t, seq, "mlstm_in_proj",
                             tn=2048, trans_w=True)
    gates_row = gates[:, :2 * A_HEADS].reshape(bn, seq, 2, A_HEADS).transpose(0, 2, 3, 1)
    gates_row = gates_row.reshape(bn, 2, A_HEADS, 1, seq)
    x2 = _mlstm_call(proj.reshape(bn, seq, n_main), gates_row, a_gate_b[0], a_conv_w[0],
                     a_conv_b[0].reshape(1, 2 * A_QK), a_g_head[0].reshape(1, A_INNER),
                     a_w_out[0].astype(BF16), x2.reshape(bn, seq, d), g_post[0].reshape(1, d), mod0).reshape(t, d)

    mod1 = _mod_call(c8, ada_w, ada_b, layer=1).reshape(SUBLANES, 1, 3 * d)
    kmod = _mod_call(c8, kv_ada_w, kv_ada_b).reshape(SUBLANES, 1, 2 * d)
    w_qzkv = _cast_pair_call(b_w_in[0], kv_w)
    qzkv = _proj_call(x2, [(g_pre[1].reshape(1, d), mod1), (kv_g.reshape(1, d), kmod)], w_qzkv, None, seq,
                      "attn_qzkv_proj", tn=1024, n1=2 * B_WIDTH,
                      lead_cols=B_WIDTH, lead_scale=(B_DH ** -0.5) * LOG2E)
    wb = _bias_call(b_rel[0])
    y1 = _attn_call(qzkv.reshape(bn, seq, 4 * B_WIDTH), wb)
    x2 = _out_call(y1.reshape(t, B_WIDTH), b_w_out[0].astype(BF16), x2, g_post[1].reshape(1, d),
                   mod1, 2, seq, "attn_out_proj")
    return x2.reshape(bn, seq, d)
```

```python
import functools

import jax
import jax.numpy as jnp
from jax import lax
from jax.experimental import pallas as pl
from jax.experimental.pallas import tpu as pltpu

F32 = jnp.float32
BF16 = jnp.bfloat16

EPS = 1e-6
CHUNK = 64
A_HEADS = 8
A_DK = 256
A_DV = 512
A_QK = A_HEADS * A_DK
A_INNER = A_HEADS * A_DV
CONV_W = 4
B_HEADS = 16
B_DH = 128
B_WIDTH = B_HEADS * B_DH
LEFT_CHUNKS = 8
PAD = LEFT_CHUNKS * CHUNK
MAX_REL = 128
NEG = -1e30

LANES = 128
SUBLANES = 8

LOG2E = 1.4426950408889634

MLSTM_CHUNK = 256
MLSTM_HPG = 8
ATTN_LQ = 256
ATTN_HPG = 8
ATTN_WK = PAD + ATTN_LQ
ATTN_NB = (PAD + ATTN_WK) // LANES

VMEM_LIMIT = 56 * 1024 * 1024


def _sigmoid(v):
    return 0.5 * jnp.tanh(0.5 * v) + 0.5


def _silu(v):
    hv = 0.5 * v
    return hv * jnp.tanh(hv) + hv


def _log_sigmoid(v):
    return -(jnp.maximum(-v, 0.0) + jnp.log1p(jnp.exp(-jnp.abs(v))))


def _mod_kernel(c_ref, w_ref, b_ref, o_ref):
    sc = _silu(c_ref[...])
    o_ref[...] = jnp.dot(sc.astype(BF16), w_ref[...].astype(BF16),
                         preferred_element_type=F32) + b_ref[...]


def _mod_call(c8, w, b, layer=None):
    d = c8.shape[1]
    n = w.shape[-1]
    tn = 1024
    if layer is None:
        w_spec = pl.BlockSpec((d, tn), lambda j: (0, j))
        b_spec = pl.BlockSpec((1, tn), lambda j: (0, j))
        b2 = b.reshape(1, n)
    else:
        w_spec = pl.BlockSpec((None, d, tn), lambda j: (layer, 0, j))
        b_spec = pl.BlockSpec((None, 1, tn), lambda j: (layer, 0, j))
        b2 = b.reshape(b.shape[0], 1, n)
    return pl.pallas_call(
        _mod_kernel,
        grid=(n // tn,),
        in_specs=[pl.BlockSpec((SUBLANES, d), lambda j: (0, 0)), w_spec, b_spec],
        out_specs=pl.BlockSpec((SUBLANES, tn), lambda j: (0, j)),
        out_shape=jax.ShapeDtypeStruct((SUBLANES, n), F32),
        compiler_params=pltpu.CompilerParams(
            dimension_semantics=("arbitrary",), vmem_limit_bytes=VMEM_LIMIT),
        name="adaln_mod",
    )(c8, w, b2)


def _cast_kernel(w_ref, wg_ref, o_ref, og_ref):
    o_ref[...] = w_ref[...].astype(BF16)

    @pl.when(pl.program_id(0) == 0)
    def _():
        r = lax.broadcasted_iota(jnp.int32, wg_ref.shape, 0)
        og_ref[...] = jnp.where(r < 2 * A_HEADS, wg_ref[...], 0.0).astype(BF16)


def _cast_w_in_call(w_t, n_main):
    _, d = w_t.shape
    tn = 1024
    return pl.pallas_call(
        _cast_kernel,
        grid=(n_main // tn,),
        in_specs=[
            pl.BlockSpec((tn, d), lambda j: (j, 0)),
            pl.BlockSpec((LANES, d), lambda j: (n_main // LANES, 0)),
        ],
        out_specs=[
            pl.BlockSpec((tn, d), lambda j: (j, 0)),
            pl.BlockSpec((LANES, d), lambda j: (0, 0)),
        ],
        out_shape=[jax.ShapeDtypeStruct((n_main, d), BF16), jax.ShapeDtypeStruct((LANES, d), BF16)],
        compiler_params=pltpu.CompilerParams(
            dimension_semantics=("arbitrary",), vmem_limit_bytes=VMEM_LIMIT),
        name="cast_w_in",
    )(w_t, w_t)


def _cast_pair_kernel(w1_ref, w2_ref, o_ref, *, n1_tiles):
    @pl.when(pl.program_id(0) < n1_tiles)
    def _():
        o_ref[...] = w1_ref[...].astype(BF16)

    @pl.when(pl.program_id(0) >= n1_tiles)
    def _():
        o_ref[...] = w2_ref[...].astype(BF16)


def _cast_pair_call(w1, w2):
    d, n1 = w1.shape
    n2 = w2.shape[1]
    tn = 512
    n1_tiles = n1 // tn
    return pl.pallas_call(
        functools.partial(_cast_pair_kernel, n1_tiles=n1_tiles),
        grid=((n1 + n2) // tn,),
        in_specs=[
            pl.BlockSpec((d, tn), lambda j: (0, jnp.minimum(j, n1_tiles - 1))),
            pl.BlockSpec((d, tn), lambda j: (0, jnp.maximum(j - n1_tiles, 0))),
        ],
        out_specs=pl.BlockSpec((d, tn), lambda j: (0, j)),
        out_shape=jax.ShapeDtypeStruct((d, n1 + n2), BF16),
        compiler_params=pltpu.CompilerParams(
            dimension_semantics=("arbitrary",), vmem_limit_bytes=VMEM_LIMIT),
        name="cast_w_qzkv",
    )(w1, w2)


_NT =(((1,), (1,)), ((), ()))


def _modulated(xn, g_ref, shift_ref, scale_ref):
    return ((xn * g_ref[...]) * (1.0 + scale_ref[...]) + shift_ref[...]).astype(BF16)


def _proj_kernel(*refs, n_mod, trans_w, with_gates, n1_tiles, lead_tiles, lead_scale):
    x_ref = refs[0]
    mod_refs = refs[1:1 + 3 * n_mod]
    rest = list(refs[1 + 3 * n_mod:])
    w_ref = rest.pop(0)
    wg_ref = rest.pop(0) if with_gates else None
    o_ref = rest.pop(0)
    og_ref = rest.pop(0) if with_gates else None
    h_scr = rest.pop(0)
    j = pl.program_id(1)

    def project(h):
        if trans_w:
            acc = lax.dot_general(h, w_ref[...], _NT, preferred_element_type=F32)
        else:
            acc = jnp.dot(h, w_ref[...], preferred_element_type=F32)
        if lead_tiles:
            acc = acc * jnp.where(j < lead_tiles, lead_scale, 1.0)
        o_ref[...] = acc.astype(o_ref.dtype)

    @pl.when(j == 0)
    def _():
        x = x_ref[...]
        xn = x * lax.rsqrt(jnp.mean(x * x, axis=-1, keepdims=True) + EPS)
        hs = []
        for k in range(n_mod):
            hb = _modulated(xn, *mod_refs[3 * k:3 * k + 3])
            h_scr[k] = hb
            hs.append(hb)
        if with_gates:
            og_ref[...] = lax.dot_general(hs[0], wg_ref[...], _NT, preferred_element_type=F32)
        project(hs[0])

    @pl.when(j > 0)
    def _():
        project(h_scr[jnp.where(j >= n1_tiles, 1, 0) if n_mod == 2 else 0])


def _proj_call(x2, mods, w, wg, seq, name, *, tn, trans_w=False, n1=None, lead_cols=0, lead_scale=1.0):
    t, d = x2.shape
    n = w.shape[0] if trans_w else w.shape[1]
    tm = 1024
    nt, nj = t // tm, n // tn
    per_batch = seq // tm
    n_mod = len(mods)
    n1 = n if n1 is None else n1
    assert lead_cols % tn == 0 and n1 % tn == 0 and n % tn == 0 and n1 >= tn

    vec = lambda blk: pl.BlockSpec((None, 1, d), lambda i, j: (i // per_batch, 0, blk))
    in_specs = [pl.BlockSpec((tm, d), lambda i, j: (i, 0))]
    args = [x2]
    for g, mod3 in mods:
        in_specs += [pl.BlockSpec((1, d), lambda i, j: (0, 0)), vec(0), vec(1)]
        args += [g, mod3, mod3]
    in_specs.append(pl.BlockSpec((tn, d), lambda i, j: (j, 0)) if trans_w
                    else pl.BlockSpec((d, tn), lambda i, j: (0, j)))
    args.append(w)
    out_specs = pl.BlockSpec((tm, tn), lambda i, j: (i, j))
    out_shape = jax.ShapeDtypeStruct((t, n), BF16)
    if wg is not None:
        in_specs.append(pl.BlockSpec((LANES, d), lambda i, j: (0, 0)))
        args.append(wg)
        out_specs = [out_specs, pl.BlockSpec((tm, LANES), lambda i, j: (i, 0))]
        out_shape = [out_shape, jax.ShapeDtypeStruct((t, LANES), F32)]
    return pl.pallas_call(
        functools.partial(_proj_kernel, n_mod=n_mod, trans_w=trans_w, with_gates=wg is not None,
                          n1_tiles=n1 // tn, lead_tiles=lead_cols // tn, lead_scale=lead_scale),
        grid=(nt, nj),
        in_specs=in_specs,
        out_specs=out_specs,
        out_shape=out_shape,
        scratch_shapes=[pltpu.VMEM((n_mod, tm, d), BF16)],
        compiler_params=pltpu.CompilerParams(
            dimension_semantics=("arbitrary", "arbitrary"), vmem_limit_bytes=VMEM_LIMIT),
        name=name,
    )(*args)


def _mlstm_kernel(gb_ref, sh_ref, q_ref, k_ref, v_ref, o_ref, z_ref, gr_ref, cwq_ref, cwk_ref,
                  cbq_ref, cbk_ref, gh_ref, wout_ref, x_ref, gpost_ref, gate_ref, xo_ref,
                  c_scr, n_scr, m_scr, hq_scr, hk_scr):
    group = pl.program_id(1)
    L = q_ref.shape[0]
    hp = q_ref.shape[1] // A_DK

    @pl.when(pl.program_id(2) == 0)
    def _():
        c_scr[...] = jnp.zeros_like(c_scr)
        n_scr[...] = jnp.zeros_like(n_scr)
        m_scr[...] = jnp.zeros_like(m_scr)
        hq_scr[...] = jnp.zeros_like(hq_scr)
        hk_scr[...] = jnp.zeros_like(hk_scr)

    row8 = lax.broadcasted_iota(jnp.int32, (SUBLANES, LANES), 0)

    def conv_silu(u_ref, halo_scr, w_ref, b_ref):
        blocks = []
        for cb in range(u_ref.shape[1] // LANES):
            cols = slice(cb * LANES, (cb + 1) * LANES)
            ub = u_ref[:, cols]
            u = ub.astype(F32)
            shifted = jnp.dot(sh_ref[...], ub, preferred_element_type=F32)
            halo = halo_scr[:, cols]
            acc = b_ref[:, cols]
            for j in range(CONV_W - 1):
                s = CONV_W - 1 - j
                blk = shifted[(s - 1) * L:s * L]
                fix = jnp.where(row8 < s, pltpu.roll(halo, s, 0), 0.0)
                blk = jnp.concatenate([blk[:SUBLANES] + fix, blk[SUBLANES:]], axis=0)
                acc = acc + blk * w_ref[j:j + 1, cols]
            acc = acc + u * w_ref[CONV_W - 1:CONV_W, cols]
            halo_scr[:, cols] = u[L - SUBLANES:]
            blocks.append(_silu(acc))
        bph = A_DK // LANES
        return [jnp.concatenate(blocks[h * bph:(h + 1) * bph], axis=1) for h in range(len(blocks) // bph)]

    q_heads = conv_silu(q_ref, hq_scr, cwq_ref, cbq_ref)
    k_heads = [kh * (A_DK ** -0.5) for kh in conv_silu(k_ref, hk_scr, cwk_ref, cbk_ref)]

    row = lax.broadcasted_iota(jnp.int32, (L, L), 0)
    col = lax.broadcasted_iota(jnp.int32, (L, L), 1)
    causal = col <= row
    eye = col == row

    for h in range(hp):
        head = group * hp + h
        q = q_heads[h]
        k = k_heads[h]
        vcols = slice(h * A_DV, (h + 1) * A_DV)
        vb = v_ref[:, vcols]

        li_row = gr_ref[0, h] + gb_ref[head]
        lf_row = _log_sigmoid(gr_ref[1, h] + gb_ref[A_HEADS + head])

        b_col = jnp.sum(jnp.where(causal, lf_row, 0.0), axis=1, keepdims=True)
        b_row = jnp.sum(jnp.where(eye, b_col, 0.0), axis=0, keepdims=True)
        a_row = li_row - b_row
        dm = jnp.where(causal, b_col + a_row, NEG)
        m_prev = m_scr[h]
        inter = b_col + m_prev
        m_t = jnp.maximum(inter, jnp.max(dm, axis=1, keepdims=True))
        w_intra = jnp.exp(dm - m_t)
        w_inter = jnp.exp(inter - m_t)

        qb = q.astype(BF16)
        kb = k.astype(BF16)
        s = lax.dot_general(qb, kb, (((1,), (1,)), ((), ())), preferred_element_type=F32) * w_intra
        c_old = c_scr[h]
        n_old = n_scr[h]
        num = (w_inter * jnp.dot(qb, c_old.astype(BF16), preferred_element_type=F32)
               + jnp.dot(s.astype(BF16), vb, preferred_element_type=F32))
        den = (w_inter * jnp.sum(q * n_old, axis=1, keepdims=True)
               + jnp.sum(s, axis=1, keepdims=True))
        r = 1.0 / jnp.maximum(jnp.abs(den), jnp.exp(-m_t))
        rn = r * lax.rsqrt(r * r * jnp.mean(num * num, axis=-1, keepdims=True) + EPS)
        hn = num * rn * gh_ref[:, vcols]
        gate = _sigmoid(o_ref[:, vcols]) * _silu(z_ref[:, vcols])
        y = hn * gate.astype(F32)
        if h % 2 == 0:
            y_even = y.astype(BF16)
        else:
            pair = jnp.concatenate([y_even, y.astype(BF16)], axis=1)
            part = jnp.dot(pair, wout_ref[(h - 1) * A_DV:(h + 1) * A_DV, :], preferred_element_type=F32)
            mixed = part if h == 1 else mixed + part

        b_last = b_col[L - 1:L, :]
        g_row = b_last + a_row
        m_new = jnp.maximum(b_last + m_prev, jnp.max(g_row, axis=1, keepdims=True))
        decay = jnp.exp(b_last + m_prev - m_new)
        ws_row = jnp.exp(g_row - m_new)
        ws_col = jnp.sum(jnp.where(eye, ws_row, 0.0), axis=1, keepdims=True)
        kw = k * ws_col
        c_scr[h] = decay * c_old + lax.dot_general(
            kw.astype(BF16), vb, (((0,), (0,)), ((), ())), preferred_element_type=F32)
        n_scr[h] = decay * n_old + jnp.sum(kw, axis=0, keepdims=True)
        m_scr[h] = m_new

    yn = mixed * lax.rsqrt(jnp.mean(mixed * mixed, axis=-1, keepdims=True) + EPS) * gpost_ref[...]
    xo_ref[...] = x_ref[...] + gate_ref[...] * yn


def _shift_matrix(L):
    t = jnp.arange(L)
    blocks = [(t[:, None] - s == t[None, :]) for s in range(1, CONV_W)]
    return jnp.concatenate(blocks, axis=0).astype(BF16)


def _mlstm_call(proj, gates_row, gate_b, conv_w, conv_b, g_head, w_out, x3, g_post, mod3):
    bn, seq, _ = proj.shape
    d = x3.shape[2]
    L = MLSTM_CHUNK
    hp = MLSTM_HPG
    assert hp == A_HEADS
    ng = A_HEADS // hp
    qw, vw = hp * A_DK, hp * A_DV
    v0 = 2 * A_QK // vw
    return pl.pallas_call(
        _mlstm_kernel,
        grid=(bn, ng, seq // L),
        in_specs=[
            pl.BlockSpec(memory_space=pltpu.SMEM),
            pl.BlockSpec(((CONV_W - 1) * L, L), lambda b, g, i: (0, 0)),
            pl.BlockSpec((None, L, qw), lambda b, g, i: (b, i, g)),
            pl.BlockSpec((None, L, qw), lambda b, g, i: (b, i, ng + g)),
            pl.BlockSpec((None, L, vw), lambda b, g, i: (b, i, v0 + g)),
            pl.BlockSpec((None, L, vw), lambda b, g, i: (b, i, v0 + ng + g)),
            pl.BlockSpec((None, L, vw), lambda b, g, i: (b, i, v0 + 2 * ng + g)),
            pl.BlockSpec((None, 2, hp, 1, L), lambda b, g, i: (b, 0, g, 0, i)),
            pl.BlockSpec((CONV_W, qw), lambda b, g, i: (0, g)),
            pl.BlockSpec((CONV_W, qw), lambda b, g, i: (0, ng + g)),
            pl.BlockSpec((1, qw), lambda b, g, i: (0, g)),
            pl.BlockSpec((1, qw), lambda b, g, i: (0, ng + g)),
            pl.BlockSpec((1, vw), lambda b, g, i: (0, g)),
            pl.BlockSpec((A_INNER, d), lambda b, g, i: (0, 0), pipeline_mode=pl.Buffered(1)),
            pl.BlockSpec((None, L, d), lambda b, g, i: (b, i, 0)),
            pl.BlockSpec((1, d), lambda b, g, i: (0, 0)),
            pl.BlockSpec((None, 1, d), lambda b, g, i: (b, 0, 2)),
        ],
        out_specs=pl.BlockSpec((None, L, d), lambda b, g, i: (b, i, 0)),
        out_shape=jax.ShapeDtypeStruct((bn, seq, d), F32),
        scratch_shapes=[
            pltpu.VMEM((hp, A_DK, A_DV), F32),
            pltpu.VMEM((hp, 1, A_DK), F32),
            pltpu.VMEM((hp, 1, 1), F32),
            pltpu.VMEM((SUBLANES, qw), F32),
            pltpu.VMEM((SUBLANES, qw), F32),
        ],
        compiler_params=pltpu.CompilerParams(
            dimension_semantics=("arbitrary", "arbitrary", "arbitrary"), vmem_limit_bytes=VMEM_LIMIT),
        name="mlstm_core",
    )(gate_b, _shift_matrix(L), proj, proj, proj, proj, proj, gates_row, conv_w, conv_w, conv_b, conv_b, g_head,
      w_out, x3, g_post, mod3)


def _out_kernel(y_ref, w_ref, x_ref, g_ref, gate_ref, o_ref):
    y = jnp.dot(y_ref[...], w_ref[...], preferred_element_type=F32)
    yn = y * lax.rsqrt(jnp.mean(y * y, axis=-1, keepdims=True) + EPS) * g_ref[...]
    o_ref[...] = x_ref[...] + gate_ref[...] * yn


def _out_call(y2, w, x2, g, mod3, gate_blk, seq, name):
    t, kdim = y2.shape
    d = w.shape[1]
    tm = 512
    per_batch = seq // tm
    return pl.pallas_call(
        _out_kernel,
        grid=(t // tm,),
        in_specs=[
            pl.BlockSpec((tm, kdim), lambda i: (i, 0)),
            pl.BlockSpec((kdim, d), lambda i: (0, 0), pipeline_mode=pl.Buffered(1)),
            pl.BlockSpec((tm, d), lambda i: (i, 0)),
            pl.BlockSpec((1, d), lambda i: (0, 0)),
            pl.BlockSpec((None, 1, d), lambda i: (i // per_batch, 0, gate_blk)),
        ],
        out_specs=pl.BlockSpec((tm, d), lambda i: (i, 0)),
        out_shape=jax.ShapeDtypeStruct((t, d), F32),
        compiler_params=pltpu.CompilerParams(
            dimension_semantics=("arbitrary",), vmem_limit_bytes=VMEM_LIMIT),
        name=name,
    )(y2, w, x2, g, mod3)


def _bias_kernel(tbl_ref, o_ref, base_scr):
    head = pl.program_id(0)
    nb, lq, _ = o_ref.shape
    width = nb * LANES + lq

    @pl.when(head == 0)
    def _():
        u = lax.broadcasted_iota(jnp.int32, (tbl_ref.shape[1], width), 1)
        u = jnp.where(u >= nb * LANES, u - width, u)
        idx = jnp.clip(PAD - u, -MAX_REL, MAX_REL) + MAX_REL
        c = lax.broadcasted_iota(jnp.int32, (tbl_ref.shape[1], width), 0)
        onehot = jnp.where(c == idx, 1.0, 0.0)
        base_scr[...] = jnp.dot(tbl_ref[...], onehot, preferred_element_type=F32,
                                precision=lax.Precision.HIGHEST)

    base = base_scr[pl.ds(head, 1), :]
    tile = jnp.broadcast_to(base, (lq, width))
    r = lax.broadcasted_iota(jnp.int32, (lq, width), 0)
    tile = pltpu.roll(tile, 0, 1, stride=1, stride_axis=0)
    uu = lax.broadcasted_iota(jnp.int32, (lq, width), 1)
    cq = r // CHUNK
    ck = uu // CHUNK - LEFT_CHUNKS
    visible = (ck <= cq) & (ck >= cq - LEFT_CHUNKS)
    tile = jnp.where(visible, tile * LOG2E, NEG)
    for j in range(nb):
        o_ref[j] = tile[:, j * LANES:(j + 1) * LANES]


def _bias_call(rel_table):
    nrel = rel_table.shape[1]
    ncols = -(-nrel // LANES) * LANES
    tbl = jnp.pad(rel_table, ((0, 0), (0, ncols - nrel)))
    return pl.pallas_call(
        _bias_kernel,
        grid=(B_HEADS,),
        in_specs=[pl.BlockSpec((B_HEADS, ncols), lambda h: (0, 0))],
        out_specs=pl.BlockSpec((None, ATTN_NB, ATTN_LQ, LANES), lambda h: (h, 0, 0, 0)),
        out_shape=jax.ShapeDtypeStruct((B_HEADS, ATTN_NB, ATTN_LQ, LANES), F32),
        scratch_shapes=[pltpu.VMEM((B_HEADS, ATTN_NB * LANES + ATTN_LQ), F32)],
        compiler_params=pltpu.CompilerParams(
            dimension_semantics=("arbitrary",), vmem_limit_bytes=VMEM_LIMIT),
        name="relpos_bias",
    )(tbl)


def _attn_kernel(q_ref, z_ref, k_ref, v_ref, wb_ref, y_ref):
    lq = q_ref.shape[0]
    wk = PAD + lq
    start = pl.program_id(2) * lq
    wstart = pl.multiple_of(jnp.maximum(start - PAD, 0), LANES)
    off = jnp.maximum(PAD - start, 0) // LANES
    for h in range(q_ref.shape[1] // B_DH):
        cols = slice(h * B_DH, (h + 1) * B_DH)
        kwin = k_ref[pl.ds(wstart, wk), cols]
        vwin = v_ref[pl.ds(wstart, wk), cols]
        s = lax.dot_general(q_ref[:, cols], kwin, (((1,), (1,)), ((), ())), preferred_element_type=F32)
        s = s + jnp.concatenate([wb_ref[h, off + j] for j in range(wk // LANES)], axis=1)
        m = jnp.max(s, axis=1, keepdims=True)
        p = jnp.exp2(s - m)
        inv = 1.0 / jnp.sum(p, axis=1, keepdims=True)
        o = jnp.dot(p.astype(BF16), vwin, preferred_element_type=F32) * inv
        y_ref[:, cols] = (o * _silu(z_ref[:, cols].astype(F32))).astype(y_ref.dtype)


def _attn_call(qzkv, wb):
    bn, seq, _ = qzkv.shape
    lq = ATTN_LQ
    gw = ATTN_HPG * B_DH
    ng = B_HEADS // ATTN_HPG
    resident = pl.Buffered(1)
    return pl.pallas_call(
        _attn_kernel,
        grid=(ng, bn, seq // lq),
        in_specs=[
            pl.BlockSpec((None, lq, gw), lambda g, b, i: (b, i, g)),
            pl.BlockSpec((None, lq, gw), lambda g, b, i: (b, i, ng + g)),
            pl.BlockSpec((None, seq, gw), lambda g, b, i: (b, 0, 2 * ng + g)),
            pl.BlockSpec((None, seq, gw), lambda g, b, i: (b, 0, 3 * ng + g)),
            pl.BlockSpec((ATTN_HPG, ATTN_NB, lq, LANES), lambda g, b, i: (g, 0, 0, 0), pipeline_mode=resident),
        ],
        out_specs=pl.BlockSpec((None, lq, gw), lambda g, b, i: (b, i, g)),
        out_shape=jax.ShapeDtypeStruct((bn, seq, B_WIDTH), BF16),
        compiler_params=pltpu.CompilerParams(
            dimension_semantics=("arbitrary", "arbitrary", "arbitrary"), vmem_limit_bytes=VMEM_LIMIT),
        name="band_attention",
    )(qzkv, qzkv, qzkv, qzkv, wb)


def kernel(x, c, ada_w, ada_b, g_pre, g_post, a_w_in, a_conv_w, a_conv_b, a_gate_b, a_g_head, a_w_out,
           kv_ada_w, kv_ada_b, kv_g, kv_w, b_w_in, b_rel, b_w_out):
    bn, seq, d = x.shape
    t = bn * seq
    assert bn <= SUBLANES and seq % 1024 == 0 and seq >= PAD + ATTN_LQ
    x2 = x.reshape(t, d)
    c8 = jnp.pad(c.astype(F32), ((0, SUBLANES - bn), (0, 0)))

    mod0 = _mod_call(c8, ada_w, ada_b, layer=0).reshape(SUBLANES, 1, 3 * d)
    n_main = 2 * A_QK + 3 * A_INNER
    w_main_t, w_gate_t = _cast_w_in_call(a_w_in[0].T, n_main)
    proj, gates = _proj_call(x2, [(g_pre[0].reshape(1, d), mod0)], w_main_t, w_gate_t, seq, "mlstm_in_proj",
                             tn=2048, trans_w=True)
    gates_row = gates[:, :2 * A_HEADS].reshape(bn, seq, 2, A_HEADS).transpose(0, 2, 3, 1)
    gates_row = gates_row.reshape(bn, 2, A_HEADS, 1, seq)
    x2 = _mlstm_call(proj.reshape(bn, seq, n_main), gates_row, a_gate_b[0], a_conv_w[0],
                     a_conv_b[0].reshape(1, 2 * A_QK), a_g_head[0].reshape(1, A_INNER),
                     a_w_out[0].astype(BF16), x2.reshape(bn, seq, d), g_post[0].reshape(1, d), mod0).reshape(t, d)

    mod1 = _mod_call(c8, ada_w, ada_b, layer=1).reshape(SUBLANES, 1, 3 * d)
    kmod = _mod_call(c8, kv_ada_w, kv_ada_b).reshape(SUBLANES, 1, 2 * d)
    w_qzkv = _cast_pair_call(b_w_in[0], kv_w)
    qzkv = _proj_call(x2, [(g_pre[1].reshape(1, d), mod1), (kv_g.reshape(1, d), kmod)], w_qzkv, None, seq,
                      "attn_qzkv_proj", tn=1024, n1=2 * B_WIDTH,
                      lead_cols=B_WIDTH, lead_scale=(B_DH ** -0.5) * LOG2E)
    wb = _bias_call(b_rel[0])
    y1 = _attn_call(qzkv.reshape(bn, seq, 4 * B_WIDTH), wb)
    x2 = _out_call(y1.reshape(t, B_WIDTH), b_w_out[0].astype(BF16), x2, g_post[1].reshape(1, d),
                   mod1, 2, seq, "attn_out_proj")
    return x2.reshape(bn, seq, d)
```
